```python
import math
import jax, jax.numpy as jnp
from jax import lax
import numpy as np

D_MODEL = 1024
BATCH = 16
SEQ = 4096
DEPTH = 2

N_EVEN = (DEPTH + 1) // 2
N_ODD = DEPTH // 2

SSM_WIDTH = D_MODEL // 4
SSM_GROUP = 16
SSM_GROUPS = SSM_WIDTH // SSM_GROUP
SSM_STATE = 64
GMLP_WIDTH = D_MODEL - SSM_WIDTH
GMLP_HEAD = 128
GMLP_HEADS = GMLP_WIDTH // GMLP_HEAD
CHUNK = 128
EVEN_IN = SSM_WIDTH + 2 * GMLP_WIDTH

CONV_WIDTH = 3
D_FF = 2816
EPS = 1e-6
DT_MIN = 1e-3
DT_MAX = 1e-1
LAMBDA_RE_MAX = -1e-4
RESID_SCALE = (2 * DEPTH) ** -0.5

kernel_name = "hybrid_s5_gmlp_shortconv_convffn"


def rmsnorm(x, g):
    xf = x.astype(jnp.float32)
    y = xf * lax.rsqrt(jnp.mean(xf * xf, axis=-1, keepdims=True) + EPS)
    return (y * g.astype(jnp.float32)).astype(x.dtype)


def causal_dwconv(x, w, b):
    k_w = w.shape[0]
    s = x.shape[1]
    xp = jnp.pad(x, ((0, 0), (k_w - 1, 0), (0, 0)))
    return b + sum(w[k] * xp[:, k:k + s] for k in range(k_w))


def s5_mixer(u, lam_re, lam_im, log_dt, b_re, b_im, c_re, c_im, d_skip, w_glu, b_glu):
    bsz, s, _ = u.shape
    uf = u.astype(jnp.float32).reshape(bsz, s, SSM_GROUPS, SSM_GROUP)
    lr = jnp.minimum(lam_re.astype(jnp.float32), LAMBDA_RE_MAX)
    li = lam_im.astype(jnp.float32)
    dt = jnp.exp(log_dt.astype(jnp.float32))[:, None]
    mag = jnp.exp(lr * dt)
    ab_re = mag * jnp.cos(li * dt)
    ab_im = mag * jnp.sin(li * dt)
    den = lr * lr + li * li
    nr = ab_re - 1.0
    ni = ab_im
    z_re = ((nr * lr + ni * li) / den)[..., None]
    z_im = ((ni * lr - nr * li) / den)[..., None]
    br = b_re.astype(jnp.float32)
    bi = b_im.astype(jnp.float32)
    bb_re = z_re * br - z_im * bi
    bb_im = z_re * bi + z_im * br
    x_re = jnp.einsum('bsgh,gph->bsgp', uf, bb_re)
    x_im = jnp.einsum('bsgh,gph->bsgp', uf, bb_im)
    a_re = jnp.broadcast_to(ab_re, (1, s) + ab_re.shape)
    a_im = jnp.broadcast_to(ab_im, (1, s) + ab_im.shape)

    def combine(left, right):
        a1r, a1i, b1r, b1i = left
        a2r, a2i, b2r, b2i = right
        return (a2r * a1r - a2i * a1i,
                a2r * a1i + a2i * a1r,
                a2r * b1r - a2i * b1i + b2r,
                a2r * b1i + a2i * b1r + b2i)

    _, _, h_re, h_im = lax.associative_scan(combine, (a_re, a_im, x_re, x_im), axis=1)
    y = (jnp.einsum('bsgp,ghp->bsgh', h_re, c_re.astype(jnp.float32))
         - jnp.einsum('bsgp,ghp->bsgh', h_im, c_im.astype(jnp.float32)))
    y = (y + d_skip.astype(jnp.float32).reshape(SSM_GROUPS, SSM_GROUP) * uf).reshape(bsz, s, SSM_WIDTH)
    y = jax.nn.gelu(y)
    y = y * jax.nn.sigmoid(y @ w_glu.astype(jnp.float32) + b_glu.astype(jnp.float32))
    return y.astype(u.dtype)


def gmlp_mixer(uv, w_s, b_s, g_v):
    bsz, s, _ = uv.shape
    u, v = jnp.split(jax.nn.gelu(uv), 2, axis=-1)
    v = rmsnorm(v, g_v).reshape(bsz, s // CHUNK, CHUNK, GMLP_HEADS, GMLP_HEAD)
    mask = jnp.tril(jnp.ones((CHUNK, CHUNK), dtype=bool))
    w = jnp.where(mask, w_s, 0)
    gate = jnp.einsum('hts,bnshc->bnthc', w, v) + b_s.T[None, None, :, :, None]
    return u * gate.reshape(bsz, s, GMLP_WIDTH)


def shortconv_mixer(p, w_conv, b_conv):
    bg, cg, hx = jnp.split(p, 3, axis=-1)
    return bg * causal_dwconv(cg * hx, w_conv, b_conv)


def conv_ffn(x, w_up, w_conv, b_conv, w_down):
    h = causal_dwconv(x @ w_up, w_conv, b_conv)
    gate, val = jnp.split(h, 2, axis=-1)
    return (jax.nn.silu(gate) * val) @ w_down


def setup_inputs(seed: int = 0) -> dict:
    key = jax.random.key(seed)
    ks = iter(jax.random.split(key, 32))
    f32 = jnp.float32
    nrm = lambda shape, std: std * jax.random.normal(next(ks), shape, f32)
    d = D_MODEL
    inp = {}
    inp["x"] = nrm((BATCH, SEQ, d), 1.0)
    inp["mix_norm_g"] = 1.0 + nrm((DEPTH, d), 0.02)
    inp["ffn_norm_g"] = 1.0 + nrm((DEPTH, d), 0.02)
    inp["final_norm_g"] = 1.0 + nrm((d,), 0.02)
    inp["ev_w_in"] = nrm((N_EVEN, d, EVEN_IN), d ** -0.5)
    inp["ev_w_out"] = nrm((N_EVEN, d, d), d ** -0.5 * RESID_SCALE)
    inp["s5_lam_re"] = -0.5 + nrm((N_EVEN, SSM_GROUPS, SSM_STATE), 0.01)
    n_idx = jnp.arange(SSM_STATE, dtype=f32)
    inp["s5_lam_im"] = math.pi * n_idx + nrm((N_EVEN, SSM_GROUPS, SSM_STATE), 0.01)
    inp["s5_log_dt"] = jax.random.uniform(next(ks), (N_EVEN, SSM_GROUPS), f32,
                                          math.log(DT_MIN), math.log(DT_MAX))
    inp["s5_b_re"] = nrm((N_EVEN, SSM_GROUPS, SSM_STATE, SSM_GROUP), (2 * SSM_GROUP) ** -0.5)
    inp["s5_b_im"] = nrm((N_EVEN, SSM_GROUPS, SSM_STATE, SSM_GROUP), (2 * SSM_GROUP) ** -0.5)
    inp["s5_c_re"] = nrm((N_EVEN, SSM_GROUPS, SSM_GROUP, SSM_STATE), SSM_STATE ** -0.5)
    inp["s5_c_im"] = nrm((N_EVEN, SSM_GROUPS, SSM_GROUP, SSM_STATE), SSM_STATE ** -0.5)
    inp["s5_d"] = nrm((N_EVEN, SSM_WIDTH), 1.0)
    inp["s5_w_glu"] = nrm((N_EVEN, SSM_WIDTH, SSM_WIDTH), SSM_WIDTH ** -0.5)
    inp["s5_b_glu"] = nrm((N_EVEN, SSM_WIDTH), 0.01)
    inp["gm_w_s"] = nrm((N_EVEN, GMLP_HEADS, CHUNK, CHUNK), CHUNK ** -0.5)
    inp["gm_b_s"] = 1.0 + nrm((N_EVEN, GMLP_HEADS, CHUNK), 0.01)
    inp["gm_v_g"] = 1.0 + nrm((N_EVEN, GMLP_WIDTH), 0.02)
    inp["od_w_in"] = nrm((N_ODD, d, 3 * d), d ** -0.5)
    inp["od_conv_w"] = nrm((N_ODD, CONV_WIDTH, d), CONV_WIDTH ** -0.5)
    inp["od_conv_b"] = nrm((N_ODD, d), 0.01)
    inp["od_w_out"] = nrm((N_ODD, d, d), d ** -0.5 * RESID_SCALE)
    inp["ffn_w_up"] = nrm((DEPTH, d, 2 * D_FF), d ** -0.5)
    inp["ffn_conv_w"] = nrm((DEPTH, CONV_WIDTH, 2 * D_FF), CONV_WIDTH ** -0.5)
    inp["ffn_conv_b"] = nrm((DEPTH, 2 * D_FF), 0.01)
    inp["ffn_w_down"] = nrm((DEPTH, D_FF, d), D_FF ** -0.5 * RESID_SCALE)
    return inp


def reference(x, mix_norm_g, ffn_norm_g, final_norm_g,
              ev_w_in, ev_w_out, s5_lam_re, s5_lam_im, s5_log_dt,
              s5_b_re, s5_b_im, s5_c_re, s5_c_im, s5_d, s5_w_glu, s5_b_glu,
              gm_w_s, gm_b_s, gm_v_g,
              od_w_in, od_conv_w, od_conv_b, od_w_out,
              ffn_w_up, ffn_conv_w, ffn_conv_b, ffn_w_down):
    h = x
    for layer in range(DEPTH):
        y = rmsnorm(h, mix_norm_g[layer])
        if layer % 2 == 0:
            e = layer // 2
            p = y @ ev_w_in[e]
            a_out = s5_mixer(p[..., :SSM_WIDTH], s5_lam_re[e], s5_lam_im[e], s5_log_dt[e],
                             s5_b_re[e], s5_b_im[e], s5_c_re[e], s5_c_im[e],
                             s5_d[e], s5_w_glu[e], s5_b_glu[e])
            b_out = gmlp_mixer(p[..., SSM_WIDTH:], gm_w_s[e], gm_b_s[e], gm_v_g[e])
            mix = jnp.concatenate([a_out, b_out], axis=-1) @ ev_w_out[e]
        else:
            o = layer // 2
            mix = shortconv_mixer(y @ od_w_in[o], od_conv_w[o], od_conv_b[o]) @ od_w_out[o]
        h = h + mix
        h = h + conv_ffn(rmsnorm(h, ffn_norm_g[layer]), ffn_w_up[layer], ffn_conv_w[layer],
                         ffn_conv_b[layer], ffn_w_down[layer])
    return rmsnorm(h, final_norm_g)
```

```python
import functools
import math

import jax
import jax.numpy as jnp
from jax import lax
from jax.experimental import pallas as pl
from jax.experimental.pallas import tpu as pltpu

F32 = jnp.float32
BF16 = jnp.bfloat16

EPS = 1e-6
LAMBDA_RE_MAX = -1e-4
LANES = 128
SUBLANES = 8
HALO = SUBLANES
SSM_GROUP = 16
SSM_STATE = 64
GMLP_HEAD = 128
CHUNK = 128
VMEM_LIMIT = 56 * 1024 * 1024


def _rms(x, g):
    ms = jnp.mean(x * x, axis=-1, keepdims=True)
    return x * lax.rsqrt(ms + EPS) * g


def _gelu(x):
    c = math.sqrt(2.0 / math.pi)
    return 0.5 * x * (1.0 + jnp.tanh(c * (x + 0.044715 * (x * x * x))))


def _sigmoid(x):
    return 1.0 / (1.0 + jnp.exp(-x))


def _const_spec(shape):
    nd = len(shape)
    return pl.BlockSpec(shape, lambda *_: (0,) * nd, pipeline_mode=pl.Buffered(1))


def _conv3(work_ref, carry_ref, hp, cw, cb, col0):
    tm, w = hp.shape
    work_ref[0:HALO, :] = carry_ref[:, col0:col0 + w]
    work_ref[HALO:HALO + tm, :] = hp
    out = (cb + cw[0:1, :] * work_ref[HALO - 2:HALO - 2 + tm, :]
           + cw[1:2, :] * work_ref[HALO - 1:HALO - 1 + tm, :]
           + cw[2:3, :] * hp)
    carry_ref[:, col0:col0 + w] = work_ref[tm:tm + HALO, :]
    return out


def _ffn_kernel(*refs, tm, tf, dff, final):
    if final:
        x_ref, g_ref, wup_ref, cw_ref, cb_ref, wdn_ref, gf_ref, o_ref, carry_ref, work_ref = refs
    else:
        x_ref, g_ref, wup_ref, cw_ref, cb_ref, wdn_ref, o_ref, carry_ref, work_ref = refs

    @pl.when(pl.program_id(1) == 0)
    def _():
        carry_ref[...] = jnp.zeros_like(carry_ref)

    x = x_ref[...]
    xn = _rms(x, g_ref[...]).astype(BF16)
    for j in range(dff // tf):
        conv = []
        for part in range(2):
            c0 = part * dff + j * tf
            hp = jnp.dot(xn, wup_ref[:, c0:c0 + tf], preferred_element_type=F32)
            conv.append(_conv3(work_ref.at[part], carry_ref, hp,
                               cw_ref[:, c0:c0 + tf], cb_ref[:, c0:c0 + tf], c0))
        a = (conv[0] * _sigmoid(conv[0]) * conv[1]).astype(BF16)
        d = jnp.dot(a, wdn_ref[j * tf:(j + 1) * tf, :], preferred_element_type=F32)
        if j == 0:
            o_ref[...] = x + d
        else:
            o_ref[...] += d
    if final:
        o_ref[...] = _rms(o_ref[...], gf_ref[...])


def _ffn_call(h, g, wup, cw, cb, wdn, gf=None, *, tm=512, tf=256):
    b, s, d = h.shape
    tm = min(tm, s)
    dff = wdn.shape[0]
    final = gf is not None
    in_specs = [
        pl.BlockSpec((None, tm, d), lambda bi, ti: (bi, ti, 0)),
        _const_spec((1, d)),
        _const_spec((d, 2 * dff)),
        _const_spec((3, 2 * dff)),
        _const_spec((1, 2 * dff)),
        _const_spec((dff, d)),
    ]
    args = [h, g, wup, cw, cb, wdn]
    if final:
        in_specs.append(_const_spec((1, d)))
        args.append(gf)
    return pl.pallas_call(
        functools.partial(_ffn_kernel, tm=tm, tf=tf, dff=dff, final=final),
        grid=(b, s // tm),
        in_specs=in_specs,
        out_specs=pl.BlockSpec((None, tm, d), lambda bi, ti: (bi, ti, 0)),
        out_shape=jax.ShapeDtypeStruct((b, s, d), F32),
        scratch_shapes=[
            pltpu.VMEM((HALO, 2 * dff), F32),
            pltpu.VMEM((2, HALO + tm, tf), F32),
        ],
        compiler_params=pltpu.CompilerParams(
            dimension_semantics=("arbitrary", "arbitrary"),
            vmem_limit_bytes=VMEM_LIMIT),
        name="conv_ffn_final" if final else "conv_ffn",
    )(*args)


def _shortconv_kernel(x_ref, g_ref, win_ref, cw_ref, cb_ref, wout_ref, o_ref,
                      carry_ref, work_ref, *, tm, tf, d):
    @pl.when(pl.program_id(1) == 0)
    def _():
        carry_ref[...] = jnp.zeros_like(carry_ref)

    x = x_ref[...]
    xn = _rms(x, g_ref[...]).astype(BF16)
    for j in range(d // tf):
        c0 = j * tf
        bg = jnp.dot(xn, win_ref[:, c0:c0 + tf], preferred_element_type=F32)
        cg = jnp.dot(xn, win_ref[:, d + c0:d + c0 + tf], preferred_element_type=F32)
        hx = jnp.dot(xn, win_ref[:, 2 * d + c0:2 * d + c0 + tf], preferred_element_type=F32)
        conv = _conv3(work_ref, carry_ref, cg * hx,
                      cw_ref[:, c0:c0 + tf], cb_ref[:, c0:c0 + tf], c0)
        m = (bg * conv).astype(BF16)
        dd = jnp.dot(m, wout_ref[c0:c0 + tf, :], preferred_element_type=F32)
        if j == 0:
            o_ref[...] = x + dd
        else:
            o_ref[...] += dd


def _shortconv_call(h, g, win, cw, cb, wout, *, tm=512, tf=256):
    b, s, d = h.shape
    tm = min(tm, s)
    return pl.pallas_call(
        functools.partial(_shortconv_kernel, tm=tm, tf=tf, d=d),
        grid=(b, s // tm),
        in_specs=[
            pl.BlockSpec((None, tm, d), lambda bi, ti: (bi, ti, 0)),
            _const_spec((1, d)),
            _const_spec((d, 3 * d)),
            _const_spec((3, d)),
            _const_spec((1, d)),
            _const_spec((d, d)),
        ],
        out_specs=pl.BlockSpec((None, tm, d), lambda bi, ti: (bi, ti, 0)),
        out_shape=jax.ShapeDtypeStruct((b, s, d), F32),
        scratch_shapes=[
            pltpu.VMEM((HALO, d), F32),
            pltpu.VMEM((HALO + tm, tf), F32),
        ],
        compiler_params=pltpu.CompilerParams(
            dimension_semantics=("arbitrary", "arbitrary"),
            vmem_limit_bytes=VMEM_LIMIT),
        name="shortconv_mixer",
    )(h, g, win, cw, cb, wout)


def _s5_prep_kernel(lr_ref, li_ref, ldt_ref, br_ref, bi_ref, ar_ref, ai_ref, bbr_ref, bbi_ref):
    lr = jnp.minimum(lr_ref[...], LAMBDA_RE_MAX)
    li = li_ref[...]
    dt = jnp.exp(ldt_ref[...])
    mag = jnp.exp(lr * dt)
    ab_re = mag * jnp.cos(li * dt)
    ab_im = mag * jnp.sin(li * dt)
    den = lr * lr + li * li
    nr = ab_re - 1.0
    ni = ab_im
    z_re = (nr * lr + ni * li) / den
    z_im = (ni * lr - nr * li) / den
    br = br_ref[...]
    bi = bi_ref[...]
    ar_ref[...] = ab_re
    ai_ref[...] = ab_im
    bbr_ref[...] = z_re * br - z_im * bi
    bbi_ref[...] = z_re * bi + z_im * br


def _s5_prep(lam_re, lam_im, log_dt, b_re, b_im):
    g, p = lam_re.shape
    n = g * p
    col = lambda a: a.reshape(n, 1)
    ldt = jnp.broadcast_to(log_dt[:, None], (g, p))
    outs = pl.pallas_call(
        _s5_prep_kernel,
        out_shape=[jax.ShapeDtypeStruct((n, 1), F32), jax.ShapeDtypeStruct((n, 1), F32),
                   jax.ShapeDtypeStruct((n, SSM_GROUP), F32),
                   jax.ShapeDtypeStruct((n, SSM_GROUP), F32)],
        name="s5_discretise",
    )(col(lam_re), col(lam_im), col(ldt), b_re.reshape(n, SSM_GROUP), b_im.reshape(n, SSM_GROUP))
    return outs


def _mix0_kernel(x_ref, g_ref, win_ref, bblk_ref, ar_ref, ai_ref, cblk_ref, dsk_ref,
                 wglu_ref, bglu_ref, ws_ref, bs_ref, gv_ref, wout_ref, o_ref,
                 xn_ref, u_ref, xs_ref, hst_ref, vn_ref, mix_ref,
                 *, nb, tc, pitch, d, ssm_w, gm_w):
    rows = nb * tc
    npair = xs_ref.shape[0] // 2
    heads = gm_w // GMLP_HEAD

    @pl.when(pl.program_id(1) == 0)
    def _():
        hst_ref[...] = jnp.zeros_like(hst_ref)

    x = x_ref[...].reshape(rows, d)
    xn_ref[...] = _rms(x, g_ref[...]).astype(BF16)

    u = jnp.dot(xn_ref[...], win_ref[:, 0:ssm_w], preferred_element_type=F32)
    u_ref[...] = u
    ub = u.astype(BF16)
    for c in range(npair):
        xs = jnp.dot(ub, bblk_ref[:, c * 2 * LANES:(c + 1) * 2 * LANES],
                     preferred_element_type=F32)
        for b in range(nb):
            xs_ref[2 * c, b * pitch:b * pitch + tc, :] = xs[b * tc:(b + 1) * tc, 0:LANES]
            xs_ref[2 * c + 1, b * pitch:b * pitch + tc, :] = xs[b * tc:(b + 1) * tc, LANES:2 * LANES]

    ar = [jnp.broadcast_to(ar_ref[c], (nb, LANES)) for c in range(npair)]
    ai = [jnp.broadcast_to(ai_ref[c], (nb, LANES)) for c in range(npair)]

    def scan_step(t, h):
        new = []
        for c in range(npair):
            idx = pl.ds(t, nb, stride=pitch)
            xr = xs_ref[2 * c, idx, :]
            xi = xs_ref[2 * c + 1, idx, :]
            hr, hi = h[2 * c], h[2 * c + 1]
            nr = ar[c] * hr - ai[c] * hi + xr
            ni = ar[c] * hi + ai[c] * hr + xi
            xs_ref[2 * c, idx, :] = nr
            xs_ref[2 * c + 1, idx, :] = ni
            new += [nr, ni]
        return tuple(new)

    h0 = tuple(hst_ref[k] for k in range(2 * npair))
    h_last = lax.fori_loop(0, tc, scan_step, h0, unroll=2)
    for k in range(2 * npair):
        hst_ref[k] = h_last[k]

    for b in range(nb):
        r0 = b * pitch
        hb = jnp.concatenate([xs_ref[k, r0:r0 + tc, :] for k in range(2 * npair)],
                             axis=-1).astype(BF16)
        y = jnp.dot(hb, cblk_ref[...], preferred_element_type=F32)
        y = _gelu(y + dsk_ref[...] * u_ref[b * tc:(b + 1) * tc, :])
        gl = jnp.dot(y.astype(BF16), wglu_ref[...], preferred_element_type=F32) + bglu_ref[...]
        mix_ref[b * tc:(b + 1) * tc, 0:ssm_w] = (y * _sigmoid(gl)).astype(BF16)

    v0 = ssm_w + gm_w
    vv = _gelu(jnp.dot(xn_ref[...], win_ref[:, v0:v0 + gm_w], preferred_element_type=F32))
    vn_ref[...] = _rms(vv, gv_ref[...]).astype(BF16)
    rr = lax.broadcasted_iota(jnp.int32, (CHUNK, CHUNK), 0)
    cc = lax.broadcasted_iota(jnp.int32, (CHUNK, CHUNK), 1)
    tril = rr >= cc
    for hp in range(heads // 2):
        c0 = ssm_w + hp * 2 * GMLP_HEAD
        uu = _gelu(jnp.dot(xn_ref[...], win_ref[:, c0:c0 + 2 * GMLP_HEAD],
                           preferred_element_type=F32))
        for k in range(2):
            hh = 2 * hp + k
            w = jnp.where(tril, ws_ref[hh], 0.0).astype(BF16)
            for b in range(nb):
                for n in range(tc // CHUNK):
                    r0 = b * tc + n * CHUNK
                    gate = jnp.dot(w, vn_ref[r0:r0 + CHUNK, hh * GMLP_HEAD:(hh + 1) * GMLP_HEAD],
                                   preferred_element_type=F32) + bs_ref[hh]
                    mix_ref[r0:r0 + CHUNK, ssm_w + hh * GMLP_HEAD:ssm_w + (hh + 1) * GMLP_HEAD] = (
                        uu[r0:r0 + CHUNK, k * GMLP_HEAD:(k + 1) * GMLP_HEAD] * gate).astype(BF16)

    out = x + jnp.dot(mix_ref[...], wout_ref[...], preferred_element_type=F32)
    o_ref[...] = out.reshape(nb, tc, d)


def _mix0_call(h, g, win, bblk, ar, ai, cblk, dsk, wglu, bglu, ws, bs, gv, wout, *, nb=8, tc=128):
    b, s, d = h.shape
    ssm_w = wglu.shape[0]
    gm_w = gv.shape[1]
    nslab = bblk.shape[1] // LANES
    pitch = tc + SUBLANES
    rows = nb * tc
    heads = gm_w // GMLP_HEAD
    kern = functools.partial(_mix0_kernel, nb=nb, tc=tc, pitch=pitch, d=d, ssm_w=ssm_w, gm_w=gm_w)
    return pl.pallas_call(
        kern,
        grid=(b // nb, s // tc),
        in_specs=[
            pl.BlockSpec((nb, tc, d), lambda bi, ti: (bi, ti, 0)),
            _const_spec((1, d)),
            _const_spec(win.shape),
            _const_spec(bblk.shape),
            _const_spec(ar.shape),
            _const_spec(ai.shape),
            _const_spec(cblk.shape),
            _const_spec((1, ssm_w)),
            _const_spec(wglu.shape),
            _const_spec((1, ssm_w)),
            _const_spec(ws.shape),
            _const_spec(bs.shape),
            _const_spec((1, gm_w)),
            _const_spec(wout.shape),
        ],
        out_specs=pl.BlockSpec((nb, tc, d), lambda bi, ti: (bi, ti, 0)),
        out_shape=jax.ShapeDtypeStruct((b, s, d), F32),
        scratch_shapes=[
            pltpu.VMEM((rows, d), BF16),
            pltpu.VMEM((rows, ssm_w), F32),
            pltpu.VMEM((nslab, nb * pitch, LANES), F32),
            pltpu.VMEM((nslab, nb, LANES), F32),
            pltpu.VMEM((rows, gm_w), BF16),
            pltpu.VMEM((rows, d), BF16),
        ],
        compiler_params=pltpu.CompilerParams(
            dimension_semantics=("arbitrary", "arbitrary"),
            vmem_limit_bytes=VMEM_LIMIT),
        name="s5_gmlp_mixer",
    )(h, g, win, bblk, ar, ai, cblk, dsk, wglu, bglu, ws, bs, gv, wout)


def _s5_operands(lam_re, lam_im, log_dt, b_re, b_im, c_re, c_im):
    g, p = lam_re.shape
    hdim = b_re.shape[-1]
    n = g * p
    npair = n // LANES
    ab_re, ab_im, bb_re, bb_im = _s5_prep(lam_re, lam_im, log_dt, b_re, b_im)
    eye = jnp.eye(g, dtype=F32)

    def b_dense(bb):
        bb = bb.reshape(g, p, hdim)
        return jnp.einsum('gph,gk->ghkp', bb, eye).reshape(g * hdim, npair, LANES)

    def c_dense(cm):
        return jnp.einsum('ghp,gk->gpkh', cm, eye).reshape(npair, LANES, g * hdim)

    bblk = jnp.stack([b_dense(bb_re), b_dense(bb_im)], axis=2).reshape(g * hdim, 2 * n)
    cblk = jnp.stack([c_dense(c_re), -c_dense(c_im)], axis=1).reshape(2 * n, g * hdim)
    ar = ab_re.reshape(npair, 1, LANES)
    ai = ab_im.reshape(npair, 1, LANES)
    return bblk.astype(BF16), ar, ai, cblk.astype(BF16)


def kernel(x, mix_norm_g, ffn_norm_g, final_norm_g, ev_w_in, ev_w_out, s5_lam_re, s5_lam_im,
           s5_log_dt, s5_b_re, s5_b_im, s5_c_re, s5_c_im, s5_d, s5_w_glu, s5_b_glu, gm_w_s,
           gm_b_s, gm_v_g, od_w_in, od_conv_w, od_conv_b, od_w_out, ffn_w_up, ffn_conv_w,
           ffn_conv_b, ffn_w_down):
    depth = mix_norm_g.shape[0]
    row = lambda a: a.reshape(1, -1).astype(F32)
    h = x
    for layer in range(depth):
        if layer % 2 == 0:
            e = layer // 2
            bblk, ar, ai, cblk = _s5_operands(s5_lam_re[e], s5_lam_im[e], s5_log_dt[e],
                                              s5_b_re[e], s5_b_im[e], s5_c_re[e], s5_c_im[e])
            h = _mix0_call(h, row(mix_norm_g[layer]), ev_w_in[e].astype(BF16), bblk, ar, ai, cblk,
                           row(s5_d[e]), s5_w_glu[e].astype(BF16), row(s5_b_glu[e]),
                           gm_w_s[e], gm_b_s[e][:, :, None], row(gm_v_g[e]),
                           ev_w_out[e].astype(BF16))
        else:
            o = layer // 2
            h = _shortconv_call(h, row(mix_norm_g[layer]), od_w_in[o].astype(BF16),
                                od_conv_w[o], row(od_conv_b[o]), od_w_out[o].astype(BF16))
        gf = row(final_norm_g) if layer == depth - 1 else None
        h = _ffn_call(h, row(ffn_norm_g[layer]), ffn_w_up[layer].astype(BF16),
                      ffn_conv_w[layer], row(ffn_conv_b[layer]),
                      ffn_w_down[layer].astype(BF16), gf)
    return h
```

```python
import functools
import math

import jax
import jax.numpy as jnp
from jax import lax
from jax.experimental import pallas as pl
from jax.experimental.pallas import tpu as pltpu

F32 = jnp.float32
BF16 = jnp.bfloat16

EPS = 1e-6
LAMBDA_RE_MAX = -1e-4
LANES = 128
SUBLANES = 8
HALO = 2 * SUBLANES
ROW_CHUNK = 32
FFN_TILE = 256
FFN_GROUP = 4
MIX_TILE = 256
MIX_GROUP = 2
SSM_GROUP = 16
SSM_STATE = 64
GMLP_HEAD = 128
CHUNK = 128
VMEM_LIMIT = 56 * 1024 * 1024


def _rms(x, g):
    ms = jnp.mean(x * x, axis=-1, keepdims=True)
    return x * lax.rsqrt(ms + EPS) * g


def _gelu(x):
    c = math.sqrt(2.0 / math.pi)
    return 0.5 * x * (1.0 + jnp.tanh(c * (x + 0.044715 * (x * x * x))))


def _sigmoid(x):
    return 1.0 / (1.0 + jnp.exp(-x))


def _const_spec(shape):
    nd = len(shape)
    return pl.BlockSpec(shape, lambda *_: (0,) * nd, pipeline_mode=pl.Buffered(1))


def _load_rows_permuted(x_ref, g, xn_ref, d):
    q = x_ref.shape[1] // d
    for k in range(0, q, 2):
        xp = jnp.concatenate([x_ref[:, k * d:(k + 1) * d], x_ref[:, (k + 1) * d:(k + 2) * d]], axis=0)
        xn_ref[k * SUBLANES:(k + 2) * SUBLANES, :] = _rms(xp, g).astype(BF16)


def _store_with_halo(work, carry_ref, col0, hp):
    tm, w = hp.shape
    work[HALO:HALO + tm, :] = hp
    sub = lax.broadcasted_iota(jnp.int32, (SUBLANES, w), 0)
    for k in range(2):
        tail = work[tm + k * SUBLANES:tm + (k + 1) * SUBLANES, :]
        prev = carry_ref[k * SUBLANES:(k + 1) * SUBLANES, col0:col0 + w]
        work[k * SUBLANES:(k + 1) * SUBLANES, :] = pltpu.roll(
            jnp.where(sub == SUBLANES - 1, prev, tail), 1, 0)
    carry_ref[:, col0:col0 + w] = work[tm:tm + HALO, :]


def _conv_rows(work, cw, r, n):
    s = SUBLANES
    tiles = lambda a: a.reshape(n // s, s, a.shape[-1])
    c = (cw[3] + cw[0] * tiles(work[r:r + n, :]) + cw[1] * tiles(work[s + r:s + r + n, :])
         + cw[2] * tiles(work[2 * s + r:2 * s + r + n, :]))
    return c.reshape(n, c.shape[-1])


def _run_tiles(nt, dk, up, gate, down):
    up(0)
    for j in range(nt):
        if j + 1 < nt:
            up(j + 1)
        if j > 0 and j % dk == 0:
            down(j // dk - 1)
        gate(j)
    down((nt - 1) // dk)


def _down_group(a_ref, w_ref, acc_ref, g, nt, dk, tf):
    n = min(dk, nt - g * dk)
    dd = jnp.dot(a_ref[g % 2, :, 0:n * tf], w_ref[g * dk * tf:(g * dk + n) * tf, :],
                 preferred_element_type=F32)
    if g == 0:
        acc_ref[...] = dd
    else:
        acc_ref[...] += dd


def _ffn_kernel(*refs, tm, tf, nt, dk, d, final):
    if final:
        (x_ref, g_ref, wup_ref, cw_ref, wdn_ref, gf_ref, o_ref,
         xn_ref, carry_ref, work_ref, a_ref, acc_ref) = refs
    else:
        (x_ref, g_ref, wup_ref, cw_ref, wdn_ref, o_ref,
         xn_ref, carry_ref, work_ref, a_ref, acc_ref) = refs

    @pl.when(pl.program_id(1) == 0)
    def _():
        carry_ref[...] = jnp.zeros_like(carry_ref)

    _load_rows_permuted(x_ref, g_ref[...], xn_ref, d)

    def up(j):
        hp = jnp.dot(xn_ref[...], wup_ref[j], preferred_element_type=F32)
        _store_with_halo(work_ref.at[j % 2], carry_ref, j * 2 * tf, hp)

    def gate(j):
        work, cw = work_ref.at[j % 2], cw_ref.at[j]
        for r in range(0, tm, ROW_CHUNK):
            c = _conv_rows(work, cw, r, ROW_CHUNK)
            gt, vl = c[:, :tf], c[:, tf:]
            a_ref[(j // dk) % 2, r:r + ROW_CHUNK, (j % dk) * tf:(j % dk + 1) * tf] = (
                gt * _sigmoid(gt) * vl).astype(BF16)

    down = functools.partial(_down_group, a_ref, wdn_ref, acc_ref, nt=nt, dk=dk, tf=tf)
    _run_tiles(nt, dk, up, gate, down)

    for k in range(tm // SUBLANES):
        out = x_ref[:, k * d:(k + 1) * d] + acc_ref[k * SUBLANES:(k + 1) * SUBLANES, :]
        if final:
            out = _rms(out, gf_ref[...])
        o_ref[:, k * d:(k + 1) * d] = out


def _ffn_call(h, g, wup, cw, wdn, gf=None, *, tm=512, dk=FFN_GROUP):
    b, s, d = h.shape
    tm = min(tm, s)
    nt, tf = wup.shape[0], wup.shape[2] // 2
    q = tm // SUBLANES
    final = gf is not None
    blk = pl.BlockSpec((None, None, SUBLANES, q * d), lambda bi, ti: (bi, ti, 0, 0))
    in_specs = [blk, _const_spec((1, d)), _const_spec(wup.shape), _const_spec(cw.shape),
                _const_spec(wdn.shape)]
    args = [h.reshape(b, s // tm, SUBLANES, q * d), g, wup, cw, wdn]
    if final:
        in_specs.append(_const_spec((1, d)))
        args.append(gf)
    out = pl.pallas_call(
        functools.partial(_ffn_kernel, tm=tm, tf=tf, nt=nt, dk=dk, d=d, final=final),
        grid=(b, s // tm),
        in_specs=in_specs,
        out_specs=blk,
        out_shape=jax.ShapeDtypeStruct((b, s // tm, SUBLANES, q * d), F32),
        scratch_shapes=[
            pltpu.VMEM((tm, d), BF16),
            pltpu.VMEM((HALO, nt * 2 * tf), F32),
            pltpu.VMEM((2, HALO + tm, 2 * tf), F32),
            pltpu.VMEM((2, tm, dk * tf), BF16),
            pltpu.VMEM((tm, d), F32),
        ],
        compiler_params=pltpu.CompilerParams(
            dimension_semantics=("arbitrary", "arbitrary"),
            vmem_limit_bytes=VMEM_LIMIT),
        name="conv_ffn_final" if final else "conv_ffn",
    )(*args)
    return out.reshape(b, s, d)


def _shortconv_kernel(x_ref, g_ref, win_ref, cw_ref, wout_ref, o_ref,
                      xn_ref, carry_ref, work_ref, bg_ref, a_ref, acc_ref, *, tm, tf, nt, dk, d):
    @pl.when(pl.program_id(1) == 0)
    def _():
        carry_ref[...] = jnp.zeros_like(carry_ref)

    _load_rows_permuted(x_ref, g_ref[...], xn_ref, d)

    def up(j):
        hp = jnp.dot(xn_ref[...], win_ref[j], preferred_element_type=F32)
        bg_ref[j % 2] = hp[:, :tf]
        _store_with_halo(work_ref.at[j % 2], carry_ref, j * tf, hp[:, tf:2 * tf] * hp[:, 2 * tf:])

    def gate(j):
        work, cw = work_ref.at[j % 2], cw_ref.at[j]
        for r in range(0, tm, ROW_CHUNK):
            c = _conv_rows(work, cw, r, ROW_CHUNK)
            a_ref[(j // dk) % 2, r:r + ROW_CHUNK, (j % dk) * tf:(j % dk + 1) * tf] = (
                bg_ref[j % 2, r:r + ROW_CHUNK, :] * c).astype(BF16)

    down = functools.partial(_down_group, a_ref, wout_ref, acc_ref, nt=nt, dk=dk, tf=tf)
    _run_tiles(nt, dk, up, gate, down)

    for k in range(tm // SUBLANES):
        o_ref[:, k * d:(k + 1) * d] = (x_ref[:, k * d:(k + 1) * d]
                                       + acc_ref[k * SUBLANES:(k + 1) * SUBLANES, :])


def _shortconv_call(h, g, win, cw, wout, *, tm=512, dk=MIX_GROUP):
    b, s, d = h.shape
    tm = min(tm, s)
    nt, tf = win.shape[0], win.shape[2] // 3
    q = tm // SUBLANES
    blk = pl.BlockSpec((None, None, SUBLANES, q * d), lambda bi, ti: (bi, ti, 0, 0))
    out = pl.pallas_call(
        functools.partial(_shortconv_kernel, tm=tm, tf=tf, nt=nt, dk=dk, d=d),
        grid=(b, s // tm),
        in_specs=[blk, _const_spec((1, d)), _const_spec(win.shape), _const_spec(cw.shape),
                  _const_spec(wout.shape)],
        out_specs=blk,
        out_shape=jax.ShapeDtypeStruct((b, s // tm, SUBLANES, q * d), F32),
        scratch_shapes=[
            pltpu.VMEM((tm, d), BF16),
            pltpu.VMEM((HALO, nt * tf), F32),
            pltpu.VMEM((2, HALO + tm, tf), F32),
            pltpu.VMEM((2, tm, tf), F32),
            pltpu.VMEM((2, tm, dk * tf), BF16),
            pltpu.VMEM((tm, d), F32),
        ],
        compiler_params=pltpu.CompilerParams(
            dimension_semantics=("arbitrary", "arbitrary"),
            vmem_limit_bytes=VMEM_LIMIT),
        name="shortconv_mixer",
    )(h.reshape(b, s // tm, SUBLANES, q * d), g, win, cw, wout)
    return out.reshape(b, s, d)


def _s5_prep_kernel(lr_ref, li_ref, ldt_ref, br_ref, bi_ref, ar_ref, ai_ref, bbr_ref, bbi_ref):
    lr = jnp.minimum(lr_ref[...], LAMBDA_RE_MAX)
    li = li_ref[...]
    dt = jnp.exp(ldt_ref[...])
    mag = jnp.exp(lr * dt)
    ab_re = mag * jnp.cos(li * dt)
    ab_im = mag * jnp.sin(li * dt)
    den = lr * lr + li * li
    nr = ab_re - 1.0
    ni = ab_im
    z_re = (nr * lr + ni * li) / den
    z_im = (ni * lr - nr * li) / den
    br = br_ref[...]
    bi = bi_ref[...]
    ar_ref[...] = ab_re
    ai_ref[...] = ab_im
    bbr_ref[...] = z_re * br - z_im * bi
    bbi_ref[...] = z_re * bi + z_im * br


def _s5_prep(lam_re, lam_im, log_dt, b_re, b_im):
    g, p = lam_re.shape
    n = g * p
    col = lambda a: a.reshape(n, 1)
    ldt = jnp.broadcast_to(log_dt[:, None], (g, p))
    outs = pl.pallas_call(
        _s5_prep_kernel,
        out_shape=[jax.ShapeDtypeStruct((n, 1), F32), jax.ShapeDtypeStruct((n, 1), F32),
                   jax.ShapeDtypeStruct((n, SSM_GROUP), F32),
                   jax.ShapeDtypeStruct((n, SSM_GROUP), F32)],
        name="s5_discretise",
    )(col(lam_re), col(lam_im), col(ldt), b_re.reshape(n, SSM_GROUP), b_im.reshape(n, SSM_GROUP))
    return outs


def _mix0_kernel(x_ref, g_ref, win_ref, bblk_ref, ar_ref, ai_ref, cblk_ref, dsk_ref,
                 wglu_ref, bglu_ref, ws_ref, bs_ref, gv_ref, wout_ref, o_ref,
                 xn_ref, u_ref, xs_ref, hst_ref, vn_ref, mix_ref,
                 *, nb, tc, pitch, d, ssm_w, gm_w):
    rows = nb * tc
    npair = xs_ref.shape[0] // 2
    heads = gm_w // GMLP_HEAD

    @pl.when(pl.program_id(1) == 0)
    def _():
        hst_ref[...] = jnp.zeros_like(hst_ref)

    x = x_ref[...].reshape(rows, d)
    xn_ref[...] = _rms(x, g_ref[...]).astype(BF16)

    u = jnp.dot(xn_ref[...], win_ref[:, 0:ssm_w], preferred_element_type=F32)
    u_ref[...] = u
    ub = u.astype(BF16)
    for c in range(npair):
        xs = jnp.dot(ub, bblk_ref[:, c * 2 * LANES:(c + 1) * 2 * LANES],
                     preferred_element_type=F32)
        for b in range(nb):
            xs_ref[2 * c, b * pitch:b * pitch + tc, :] = xs[b * tc:(b + 1) * tc, 0:LANES]
            xs_ref[2 * c + 1, b * pitch:b * pitch + tc, :] = xs[b * tc:(b + 1) * tc, LANES:2 * LANES]

    ar = [jnp.broadcast_to(ar_ref[c], (nb, LANES)) for c in range(npair)]
    ai = [jnp.broadcast_to(ai_ref[c], (nb, LANES)) for c in range(npair)]

    def scan_step(t, h):
        new = []
        for c in range(npair):
            idx = pl.ds(t, nb, stride=pitch)
            xr = xs_ref[2 * c, idx, :]
            xi = xs_ref[2 * c + 1, idx, :]
            hr, hi = h[2 * c], h[2 * c + 1]
            nr = ar[c] * hr - ai[c] * hi + xr
            ni = ar[c] * hi + ai[c] * hr + xi
            xs_ref[2 * c, idx, :] = nr
            xs_ref[2 * c + 1, idx, :] = ni
            new += [nr, ni]
        return tuple(new)

    h0 = tuple(hst_ref[k] for k in range(2 * npair))
    h_last = lax.fori_loop(0, tc, scan_step, h0, unroll=2)
    for k in range(2 * npair):
        hst_ref[k] = h_last[k]

    for b in range(nb):
        r0 = b * pitch
        hb = jnp.concatenate([xs_ref[k, r0:r0 + tc, :] for k in range(2 * npair)],
                             axis=-1).astype(BF16)
        y = jnp.dot(hb, cblk_ref[...], preferred_element_type=F32)
        y = _gelu(y + dsk_ref[...] * u_ref[b * tc:(b + 1) * tc, :])
        gl = jnp.dot(y.astype(BF16), wglu_ref[...], preferred_element_type=F32) + bglu_ref[...]
        mix_ref[b * tc:(b + 1) * tc, 0:ssm_w] = (y * _sigmoid(gl)).astype(BF16)

    v0 = ssm_w + gm_w
    vv = _gelu(jnp.dot(xn_ref[...], win_ref[:, v0:v0 + gm_w], preferred_element_type=F32))
    vn_ref[...] = _rms(vv, gv_ref[...]).astype(BF16)
    rr = lax.broadcasted_iota(jnp.int32, (CHUNK, CHUNK), 0)
    cc = lax.broadcasted_iota(jnp.int32, (CHUNK, CHUNK), 1)
    tril = rr >= cc
    for hp in range(heads // 2):
        c0 = ssm_w + hp * 2 * GMLP_HEAD
        uu = _gelu(jnp.dot(xn_ref[...], win_ref[:, c0:c0 + 2 * GMLP_HEAD],
                           preferred_element_type=F32))
        for k in range(2):
            hh = 2 * hp + k
            w = jnp.where(tril, ws_ref[hh], 0.0).astype(BF16)
            for b in range(nb):
                for n in range(tc // CHUNK):
                    r0 = b * tc + n * CHUNK
                    gate = jnp.dot(w, vn_ref[r0:r0 + CHUNK, hh * GMLP_HEAD:(hh + 1) * GMLP_HEAD],
                                   preferred_element_type=F32) + bs_ref[hh]
                    mix_ref[r0:r0 + CHUNK, ssm_w + hh * GMLP_HEAD:ssm_w + (hh + 1) * GMLP_HEAD] = (
                        uu[r0:r0 + CHUNK, k * GMLP_HEAD:(k + 1) * GMLP_HEAD] * gate).astype(BF16)

    out = x + jnp.dot(mix_ref[...], wout_ref[...], preferred_element_type=F32)
    o_ref[...] = out.reshape(nb, tc, d)


def _mix0_call(h, g, win, bblk, ar, ai, cblk, dsk, wglu, bglu, ws, bs, gv, wout, *, nb=8, tc=128):
    b, s, d = h.shape
    ssm_w = wglu.shape[0]
    gm_w = gv.shape[1]
    nslab = bblk.shape[1] // LANES
    pitch = tc + SUBLANES
    rows = nb * tc
    heads = gm_w // GMLP_HEAD
    kern = functools.partial(_mix0_kernel, nb=nb, tc=tc, pitch=pitch, d=d, ssm_w=ssm_w, gm_w=gm_w)
    return pl.pallas_call(
        kern,
        grid=(b // nb, s // tc),
        in_specs=[
            pl.BlockSpec((nb, tc, d), lambda bi, ti: (bi, ti, 0)),
            _const_spec((1, d)),
            _const_spec(win.shape),
            _const_spec(bblk.shape),
            _const_spec(ar.shape),
            _const_spec(ai.shape),
            _const_spec(cblk.shape),
            _const_spec((1, ssm_w)),
            _const_spec(wglu.shape),
            _const_spec((1, ssm_w)),
            _const_spec(ws.shape),
            _const_spec(bs.shape),
            _const_spec((1, gm_w)),
            _const_spec(wout.shape),
        ],
        out_specs=pl.BlockSpec((nb, tc, d), lambda bi, ti: (bi, ti, 0)),
        out_shape=jax.ShapeDtypeStruct((b, s, d), F32),
        scratch_shapes=[
            pltpu.VMEM((rows, d), BF16),
            pltpu.VMEM((rows, ssm_w), F32),
            pltpu.VMEM((nslab, nb * pitch, LANES), F32),
            pltpu.VMEM((nslab, nb, LANES), F32),
            pltpu.VMEM((rows, gm_w), BF16),
            pltpu.VMEM((rows, d), BF16),
        ],
        compiler_params=pltpu.CompilerParams(
            dimension_semantics=("arbitrary", "arbitrary"),
            vmem_limit_bytes=VMEM_LIMIT),
        name="s5_gmlp_mixer",
    )(h, g, win, bblk, ar, ai, cblk, dsk, wglu, bglu, ws, bs, gv, wout)


def _s5_operands(lam_re, lam_im, log_dt, b_re, b_im, c_re, c_im):
    g, p = lam_re.shape
    hdim = b_re.shape[-1]
    n = g * p
    npair = n // LANES
    ab_re, ab_im, bb_re, bb_im = _s5_prep(lam_re, lam_im, log_dt, b_re, b_im)
    eye = jnp.eye(g, dtype=F32)

    def b_dense(bb):
        bb = bb.reshape(g, p, hdim)
        return jnp.einsum('gph,gk->ghkp', bb, eye).reshape(g * hdim, npair, LANES)

    def c_dense(cm):
        return jnp.einsum('ghp,gk->gpkh', cm, eye).reshape(npair, LANES, g * hdim)

    bblk = jnp.stack([b_dense(bb_re), b_dense(bb_im)], axis=2).reshape(g * hdim, 2 * n)
    cblk = jnp.stack([c_dense(c_re), -c_dense(c_im)], axis=1).reshape(2 * n, g * hdim)
    ar = ab_re.reshape(npair, 1, LANES)
    ai = ab_im.reshape(npair, 1, LANES)
    return bblk.astype(BF16), ar, ai, cblk.astype(BF16)


def kernel(x, mix_norm_g, ffn_norm_g, final_norm_g, ev_w_in, ev_w_out, s5_lam_re, s5_lam_im,
           s5_log_dt, s5_b_re, s5_b_im, s5_c_re, s5_c_im, s5_d, s5_w_glu, s5_b_glu, gm_w_s,
           gm_b_s, gm_v_g, od_w_in, od_conv_w, od_conv_b, od_w_out, ffn_w_up, ffn_conv_w,
           ffn_conv_b, ffn_w_down):
    depth = mix_norm_g.shape[0]
    row = lambda a: a.reshape(1, -1).astype(F32)

    def col_tiles(w, parts, tf):
        k, n = w.shape[0], w.shape[1] // parts
        return w.reshape(k, parts, n // tf, tf).transpose(2, 0, 1, 3).reshape(n // tf, k, parts * tf)

    def conv_tiles(cw, cb, parts, tf):
        t = col_tiles(jnp.concatenate([cw, cb[None, :]], axis=0).astype(F32), parts, tf)
        return jnp.broadcast_to(t[:, :, None, :], t.shape[:2] + (SUBLANES, t.shape[2]))

    h = x
    for layer in range(depth):
        if layer % 2 == 0:
            e = layer // 2
            bblk, ar, ai, cblk = _s5_operands(s5_lam_re[e], s5_lam_im[e], s5_log_dt[e],
                                              s5_b_re[e], s5_b_im[e], s5_c_re[e], s5_c_im[e])
            h = _mix0_call(h, row(mix_norm_g[layer]), ev_w_in[e].astype(BF16), bblk, ar, ai, cblk,
                           row(s5_d[e]), s5_w_glu[e].astype(BF16), row(s5_b_glu[e]),
                           gm_w_s[e], gm_b_s[e][:, :, None], row(gm_v_g[e]),
                           ev_w_out[e].astype(BF16))
        else:
            o = layer // 2
            h = _shortconv_call(h, row(mix_norm_g[layer]),
                                col_tiles(od_w_in[o].astype(BF16), 3, MIX_TILE),
                                conv_tiles(od_conv_w[o], od_conv_b[o], 1, MIX_TILE),
                                od_w_out[o].astype(BF16))
        gf = row(final_norm_g) if layer == depth - 1 else None
        h = _ffn_call(h, row(ffn_norm_g[layer]),
                      col_tiles(ffn_w_up[layer].astype(BF16), 2, FFN_TILE),
                      conv_tiles(ffn_conv_w[layer], ffn_conv_b[layer], 2, FFN_TILE),
                      ffn_w_down[layer].astype(BF16), gf)
    return h
```

```python
import functools
import math

import jax
import jax.numpy as jnp
from jax import lax
from jax.experimental import pallas as pl
from jax.experimental.pallas import tpu as pltpu

F32 = jnp.float32
BF16 = jnp.bfloat16

EPS = 1e-6
LAMBDA_RE_MAX = -1e-4
LANES = 128
SUBLANES = 8
HALO = 2 * SUBLANES
SUB_ROWS = SUBLANES * SUBLANES
WORK_SUB = HALO + SUB_ROWS
ROW_CHUNK = 32
FFN_TILE = 256
FFN_GROUP = 4
MIX_TILE = 256
MIX_GROUP = 2
SSM_GROUP = 16
SSM_STATE = 64
GMLP_HEAD = 128
CHUNK = 128
VMEM_LIMIT = 56 * 1024 * 1024


def _rms(x, g):
    ms = jnp.mean(x * x, axis=-1, keepdims=True)
    return x * lax.rsqrt(ms + EPS) * g


def _gelu(x):
    c = math.sqrt(2.0 / math.pi)
    return 0.5 * x * (1.0 + jnp.tanh(c * (x + 0.044715 * (x * x * x))))


def _sigmoid(x):
    return 1.0 / (1.0 + jnp.exp(-x))


def _const_spec(shape):
    nd = len(shape)
    return pl.BlockSpec(shape, lambda *_: (0,) * nd, pipeline_mode=pl.Buffered(1))


def _transposed_tile(slab_ref, sb, k):
    return slab_ref[pl.ds(sb * SUB_ROWS + k, SUBLANES, stride=SUBLANES), :]


def _load_rows_permuted(x_refs, g, xn_ref):
    for t in range(0, xn_ref.shape[0] // SUBLANES, 2):
        xp = jnp.concatenate(
            [jnp.concatenate([_transposed_tile(xr, tt // SUBLANES, tt % SUBLANES) for xr in x_refs],
                             axis=-1) for tt in (t, t + 1)], axis=0)
        xn_ref[t * SUBLANES:(t + 2) * SUBLANES, :] = _rms(xp, g).astype(BF16)


def _store_with_halo(work, carry_ref, col0, hp):
    tm, w = hp.shape
    sub = lax.broadcasted_iota(jnp.int32, (SUBLANES, w), 0)
    for sb in range(tm // SUB_ROWS):
        base = sb * WORK_SUB
        work[base + HALO:base + WORK_SUB, :] = hp[sb * SUB_ROWS:(sb + 1) * SUB_ROWS, :]
        for k in range(2):
            tail = work[base + WORK_SUB - HALO + k * SUBLANES:base + WORK_SUB - HALO + (k + 1) * SUBLANES, :]
            if sb == 0:
                prev = carry_ref[k * SUBLANES:(k + 1) * SUBLANES, col0:col0 + w]
            else:
                prev = work[base - HALO + k * SUBLANES:base - HALO + (k + 1) * SUBLANES, :]
            work[base + k * SUBLANES:base + (k + 1) * SUBLANES, :] = pltpu.roll(
                jnp.where(sub == SUBLANES - 1, prev, tail), 1, 0)
    last = (tm // SUB_ROWS) * WORK_SUB
    carry_ref[:, col0:col0 + w] = work[last - HALO:last, :]


def _conv_rows(work, cw, sb, r, n):
    s = SUBLANES
    r0 = sb * WORK_SUB + r
    tiles = lambda a: a.reshape(n // s, s, a.shape[-1])
    c = (cw[3] + cw[0] * tiles(work[r0:r0 + n, :]) + cw[1] * tiles(work[s + r0:s + r0 + n, :])
         + cw[2] * tiles(work[2 * s + r0:2 * s + r0 + n, :]))
    return c.reshape(n, c.shape[-1])


def _acc_slabs(acc_ref, dd, first):
    for c in range(acc_ref.shape[0]):
        if first:
            acc_ref[c] = dd[:, c * LANES:(c + 1) * LANES]
        else:
            acc_ref[c] += dd[:, c * LANES:(c + 1) * LANES]


def _natural_tile(acc_ref, t):
    return jnp.concatenate([_transposed_tile(acc_ref.at[c], t // SUBLANES, t % SUBLANES)
                            for c in range(acc_ref.shape[0])], axis=-1)


def _run_tiles(nt, dk, up, gate, down):
    up(0)
    for j in range(nt):
        if j + 1 < nt:
            up(j + 1)
        if j > 0 and j % dk == 0:
            down(j // dk - 1)
        gate(j)
    down((nt - 1) // dk)


def _down_group(a_ref, w_ref, acc_ref, g, nt, dk, tf):
    n = min(dk, nt - g * dk)
    dd = jnp.dot(a_ref[g % 2, :, 0:n * tf], w_ref[g * dk * tf:(g * dk + n) * tf, :],
                 preferred_element_type=F32)
    _acc_slabs(acc_ref, dd, first=(g == 0))


def _ffn_kernel(*refs, tm, tf, nt, dk, d, final):
    ns = d // LANES
    x_refs, refs = refs[:ns], refs[ns:]
    if final:
        (g_ref, wup_ref, cw_ref, wdn_ref, gf_ref, o_ref,
         xn_ref, carry_ref, work_ref, a_ref, acc_ref) = refs
    else:
        (g_ref, wup_ref, cw_ref, wdn_ref, o_ref,
         xn_ref, carry_ref, work_ref, a_ref, acc_ref) = refs

    @pl.when(pl.program_id(1) == 0)
    def _():
        carry_ref[...] = jnp.zeros_like(carry_ref)

    _load_rows_permuted(x_refs, g_ref[...], xn_ref)

    def up(j):
        hp = jnp.dot(xn_ref[...], wup_ref[j], preferred_element_type=F32)
        _store_with_halo(work_ref.at[j % 2], carry_ref, j * 2 * tf, hp)

    def gate(j):
        work, cw = work_ref.at[j % 2], cw_ref.at[j]
        for r in range(0, tm, ROW_CHUNK):
            c = _conv_rows(work, cw, r // SUB_ROWS, r % SUB_ROWS, ROW_CHUNK)
            gt, vl = c[:, :tf], c[:, tf:]
            a_ref[(j // dk) % 2, r:r + ROW_CHUNK, (j % dk) * tf:(j % dk + 1) * tf] = (
                gt * _sigmoid(gt) * vl).astype(BF16)

    down = functools.partial(_down_group, a_ref, wdn_ref, acc_ref, nt=nt, dk=dk, tf=tf)
    _run_tiles(nt, dk, up, gate, down)

    for t in range(tm // SUBLANES):
        rows = slice(t * SUBLANES, (t + 1) * SUBLANES)
        out = jnp.concatenate([xr[rows, :] for xr in x_refs], axis=-1) + _natural_tile(acc_ref, t)
        if final:
            out = _rms(out, gf_ref[...])
        o_ref[rows, :] = out


def _slab_specs(tm, d):
    return [pl.BlockSpec((None, tm, LANES), lambda bi, ti, c=c: (bi, ti, c)) for c in range(d // LANES)]


def _ffn_call(h, g, wup, cw, wdn, gf=None, *, tm=512, dk=FFN_GROUP):
    b, s, d = h.shape
    tm = min(tm, s)
    nt, tf = wup.shape[0], wup.shape[2] // 2
    ns = d // LANES
    final = gf is not None
    in_specs = _slab_specs(tm, d) + [_const_spec((1, d)), _const_spec(wup.shape),
                                     _const_spec(cw.shape), _const_spec(wdn.shape)]
    args = [h] * ns + [g, wup, cw, wdn]
    if final:
        in_specs.append(_const_spec((1, d)))
        args.append(gf)
    return pl.pallas_call(
        functools.partial(_ffn_kernel, tm=tm, tf=tf, nt=nt, dk=dk, d=d, final=final),
        grid=(b, s // tm),
        in_specs=in_specs,
        out_specs=pl.BlockSpec((None, tm, d), lambda bi, ti: (bi, ti, 0)),
        out_shape=jax.ShapeDtypeStruct((b, s, d), F32),
        scratch_shapes=[
            pltpu.VMEM((tm, d), BF16),
            pltpu.VMEM((HALO, nt * 2 * tf), F32),
            pltpu.VMEM((2, tm // SUB_ROWS * WORK_SUB, 2 * tf), F32),
            pltpu.VMEM((2, tm, dk * tf), BF16),
            pltpu.VMEM((ns, tm, LANES), F32),
        ],
        compiler_params=pltpu.CompilerParams(
            dimension_semantics=("arbitrary", "arbitrary"),
            vmem_limit_bytes=VMEM_LIMIT),
        name="conv_ffn_final" if final else "conv_ffn",
    )(*args)


def _shortconv_kernel(*refs, tm, tf, nt, dk, d):
    ns = d // LANES
    x_refs, refs = refs[:ns], refs[ns:]
    (g_ref, win_ref, cw_ref, wout_ref, o_ref,
     xn_ref, carry_ref, work_ref, bg_ref, a_ref, acc_ref) = refs

    @pl.when(pl.program_id(1) == 0)
    def _():
        carry_ref[...] = jnp.zeros_like(carry_ref)

    _load_rows_permuted(x_refs, g_ref[...], xn_ref)

    def up(j):
        hp = jnp.dot(xn_ref[...], win_ref[j], preferred_element_type=F32)
        bg_ref[j % 2] = hp[:, :tf]
        _store_with_halo(work_ref.at[j % 2], carry_ref, j * tf, hp[:, tf:2 * tf] * hp[:, 2 * tf:])

    def gate(j):
        work, cw = work_ref.at[j % 2], cw_ref.at[j]
        for r in range(0, tm, ROW_CHUNK):
            c = _conv_rows(work, cw, r // SUB_ROWS, r % SUB_ROWS, ROW_CHUNK)
            a_ref[(j // dk) % 2, r:r + ROW_CHUNK, (j % dk) * tf:(j % dk + 1) * tf] = (
                bg_ref[j % 2, r:r + ROW_CHUNK, :] * c).astype(BF16)

    down = functools.partial(_down_group, a_ref, wout_ref, acc_ref, nt=nt, dk=dk, tf=tf)
    _run_tiles(nt, dk, up, gate, down)

    for t in range(tm // SUBLANES):
        rows = slice(t * SUBLANES, (t + 1) * SUBLANES)
        o_ref[rows, :] = (jnp.concatenate([xr[rows, :] for xr in x_refs], axis=-1)
                          + _natural_tile(acc_ref, t))


def _shortconv_call(h, g, win, cw, wout, *, tm=512, dk=MIX_GROUP):
    b, s, d = h.shape
    tm = min(tm, s)
    nt, tf = win.shape[0], win.shape[2] // 3
    ns = d // LANES
    return pl.pallas_call(
        functools.partial(_shortconv_kernel, tm=tm, tf=tf, nt=nt, dk=dk, d=d),
        grid=(b, s // tm),
        in_specs=_slab_specs(tm, d) + [_const_spec((1, d)), _const_spec(win.shape),
                                       _const_spec(cw.shape), _const_spec(wout.shape)],
        out_specs=pl.BlockSpec((None, tm, d), lambda bi, ti: (bi, ti, 0)),
        out_shape=jax.ShapeDtypeStruct((b, s, d), F32),
        scratch_shapes=[
            pltpu.VMEM((tm, d), BF16),
            pltpu.VMEM((HALO, nt * tf), F32),
            pltpu.VMEM((2, tm // SUB_ROWS * WORK_SUB, tf), F32),
            pltpu.VMEM((2, tm, tf), F32),
            pltpu.VMEM((2, tm, dk * tf), BF16),
            pltpu.VMEM((ns, tm, LANES), F32),
        ],
        compiler_params=pltpu.CompilerParams(
            dimension_semantics=("arbitrary", "arbitrary"),
            vmem_limit_bytes=VMEM_LIMIT),
        name="shortconv_mixer",
    )(*([h] * ns), g, win, cw, wout)


def _s5_prep_kernel(lr_ref, li_ref, ldt_ref, br_ref, bi_ref, ar_ref, ai_ref, bbr_ref, bbi_ref):
    lr = jnp.minimum(lr_ref[...], LAMBDA_RE_MAX)
    li = li_ref[...]
    dt = jnp.exp(ldt_ref[...])
    mag = jnp.exp(lr * dt)
    ab_re = mag * jnp.cos(li * dt)
    ab_im = mag * jnp.sin(li * dt)
    den = lr * lr + li * li
    nr = ab_re - 1.0
    ni = ab_im
    z_re = (nr * lr + ni * li) / den
    z_im = (ni * lr - nr * li) / den
    br = br_ref[...]
    bi = bi_ref[...]
    ar_ref[...] = ab_re
    ai_ref[...] = ab_im
    bbr_ref[...] = z_re * br - z_im * bi
    bbi_ref[...] = z_re * bi + z_im * br


def _s5_prep(lam_re, lam_im, log_dt, b_re, b_im):
    g, p = lam_re.shape
    n = g * p
    col = lambda a: a.reshape(n, 1)
    ldt = jnp.broadcast_to(log_dt[:, None], (g, p))
    outs = pl.pallas_call(
        _s5_prep_kernel,
        out_shape=[jax.ShapeDtypeStruct((n, 1), F32), jax.ShapeDtypeStruct((n, 1), F32),
                   jax.ShapeDtypeStruct((n, SSM_GROUP), F32),
                   jax.ShapeDtypeStruct((n, SSM_GROUP), F32)],
        name="s5_discretise",
    )(col(lam_re), col(lam_im), col(ldt), b_re.reshape(n, SSM_GROUP), b_im.reshape(n, SSM_GROUP))
    return outs


def _mix0_kernel(x_ref, g_ref, win_ref, bblk_ref, ar_ref, ai_ref, cblk_ref, dsk_ref,
                 wglu_ref, bglu_ref, ws_ref, bs_ref, gv_ref, wout_ref, o_ref,
                 xn_ref, u_ref, xs_ref, hst_ref, vn_ref, mix_ref,
                 *, nb, tc, pitch, d, ssm_w, gm_w):
    rows = nb * tc
    npair = xs_ref.shape[0] // 2
    heads = gm_w // GMLP_HEAD

    @pl.when(pl.program_id(1) == 0)
    def _():
        hst_ref[...] = jnp.zeros_like(hst_ref)

    x = x_ref[...].reshape(rows, d)
    xn_ref[...] = _rms(x, g_ref[...]).astype(BF16)

    u = jnp.dot(xn_ref[...], win_ref[:, 0:ssm_w], preferred_element_type=F32)
    u_ref[...] = u
    ub = u.astype(BF16)
    for c in range(npair):
        xs = jnp.dot(ub, bblk_ref[:, c * 2 * LANES:(c + 1) * 2 * LANES],
                     preferred_element_type=F32)
        for b in range(nb):
            xs_ref[2 * c, b * pitch:b * pitch + tc, :] = xs[b * tc:(b + 1) * tc, 0:LANES]
            xs_ref[2 * c + 1, b * pitch:b * pitch + tc, :] = xs[b * tc:(b + 1) * tc, LANES:2 * LANES]

    ar = [jnp.broadcast_to(ar_ref[c], (nb, LANES)) for c in range(npair)]
    ai = [jnp.broadcast_to(ai_ref[c], (nb, LANES)) for c in range(npair)]

    def scan_step(t, h):
        new = []
        for c in range(npair):
            idx = pl.ds(t, nb, stride=pitch)
            xr = xs_ref[2 * c, idx, :]
            xi = xs_ref[2 * c + 1, idx, :]
            hr, hi = h[2 * c], h[2 * c + 1]
            nr = ar[c] * hr - ai[c] * hi + xr
            ni = ar[c] * hi + ai[c] * hr + xi
            xs_ref[2 * c, idx, :] = nr
            xs_ref[2 * c + 1, idx, :] = ni
            new += [nr, ni]
        return tuple(new)

    h0 = tuple(hst_ref[k] for k in range(2 * npair))
    h_last = lax.fori_loop(0, tc, scan_step, h0, unroll=2)
    for k in range(2 * npair):
        hst_ref[k] = h_last[k]

    for b in range(nb):
        r0 = b * pitch
        hb = jnp.concatenate([xs_ref[k, r0:r0 + tc, :] for k in range(2 * npair)],
                             axis=-1).astype(BF16)
        y = jnp.dot(hb, cblk_ref[...], preferred_element_type=F32)
        y = _gelu(y + dsk_ref[...] * u_ref[b * tc:(b + 1) * tc, :])
        gl = jnp.dot(y.astype(BF16), wglu_ref[...], preferred_element_type=F32) + bglu_ref[...]
        mix_ref[b * tc:(b + 1) * tc, 0:ssm_w] = (y * _sigmoid(gl)).astype(BF16)

    v0 = ssm_w + gm_w
    vv = _gelu(jnp.dot(xn_ref[...], win_ref[:, v0:v0 + gm_w], preferred_element_type=F32))
    vn_ref[...] = _rms(vv, gv_ref[...]).astype(BF16)
    rr = lax.broadcasted_iota(jnp.int32, (CHUNK, CHUNK), 0)
    cc = lax.broadcasted_iota(jnp.int32, (CHUNK, CHUNK), 1)
    tril = rr >= cc
    for hp in range(heads // 2):
        c0 = ssm_w + hp * 2 * GMLP_HEAD
        uu = _gelu(jnp.dot(xn_ref[...], win_ref[:, c0:c0 + 2 * GMLP_HEAD],
                           preferred_element_type=F32))
        for k in range(2):
            hh = 2 * hp + k
            w = jnp.where(tril, ws_ref[hh], 0.0).astype(BF16)
            for b in range(nb):
                for n in range(tc // CHUNK):
                    r0 = b * tc + n * CHUNK
                    gate = jnp.dot(w, vn_ref[r0:r0 + CHUNK, hh * GMLP_HEAD:(hh + 1) * GMLP_HEAD],
                                   preferred_element_type=F32) + bs_ref[hh]
                    mix_ref[r0:r0 + CHUNK, ssm_w + hh * GMLP_HEAD:ssm_w + (hh + 1) * GMLP_HEAD] = (
                        uu[r0:r0 + CHUNK, k * GMLP_HEAD:(k + 1) * GMLP_HEAD] * gate).astype(BF16)

    out = x + jnp.dot(mix_ref[...], wout_ref[...], preferred_element_type=F32)
    o_ref[...] = out.reshape(nb, tc, d)


def _mix0_call(h, g, win, bblk, ar, ai, cblk, dsk, wglu, bglu, ws, bs, gv, wout, *, nb=8, tc=128):
    b, s, d = h.shape
    ssm_w = wglu.shape[0]
    gm_w = gv.shape[1]
    nslab = bblk.shape[1] // LANES
    pitch = tc + SUBLANES
    rows = nb * tc
    heads = gm_w // GMLP_HEAD
    kern = functools.partial(_mix0_kernel, nb=nb, tc=tc, pitch=pitch, d=d, ssm_w=ssm_w, gm_w=gm_w)
    return pl.pallas_call(
        kern,
        grid=(b // nb, s // tc),
        in_specs=[
            pl.BlockSpec((nb, tc, d), lambda bi, ti: (bi, ti, 0)),
            _const_spec((1, d)),
            _const_spec(win.shape),
            _const_spec(bblk.shape),
            _const_spec(ar.shape),
            _const_spec(ai.shape),
            _const_spec(cblk.shape),
            _const_spec((1, ssm_w)),
            _const_spec(wglu.shape),
            _const_spec((1, ssm_w)),
            _const_spec(ws.shape),
            _const_spec(bs.shape),
            _const_spec((1, gm_w)),
            _const_spec(wout.shape),
        ],
        out_specs=pl.BlockSpec((nb, tc, d), lambda bi, ti: (bi, ti, 0)),
        out_shape=jax.ShapeDtypeStruct((b, s, d), F32),
        scratch_shapes=[
            pltpu.VMEM((rows, d), BF16),
            pltpu.VMEM((rows, ssm_w), F32),
            pltpu.VMEM((nslab, nb * pitch, LANES), F32),
            pltpu.VMEM((nslab, nb, LANES), F32),
            pltpu.VMEM((rows, gm_w), BF16),
            pltpu.VMEM((rows, d), BF16),
        ],
        compiler_params=pltpu.CompilerParams(
            dimension_semantics=("arbitrary", "arbitrary"),
            vmem_limit_bytes=VMEM_LIMIT),
        name="s5_gmlp_mixer",
    )(h, g, win, bblk, ar, ai, cblk, dsk, wglu, bglu, ws, bs, gv, wout)


def _s5_operands(lam_re, lam_im, log_dt, b_re, b_im, c_re, c_im):
    g, p = lam_re.shape
    hdim = b_re.shape[-1]
    n = g * p
    npair = n // LANES
    ab_re, ab_im, bb_re, bb_im = _s5_prep(lam_re, lam_im, log_dt, b_re, b_im)
    eye = jnp.eye(g, dtype=F32)

    def b_dense(bb):
        bb = bb.reshape(g, p, hdim)
        return jnp.einsum('gph,gk->ghkp', bb, eye).reshape(g * hdim, npair, LANES)

    def c_dense(cm):
        return jnp.einsum('ghp,gk->gpkh', cm, eye).reshape(npair, LANES, g * hdim)

    bblk = jnp.stack([b_dense(bb_re), b_dense(bb_im)], axis=2).reshape(g * hdim, 2 * n)
    cblk = jnp.stack([c_dense(c_re), -c_dense(c_im)], axis=1).reshape(2 * n, g * hdim)
    ar = ab_re.reshape(npair, 1, LANES)
    ai = ab_im.reshape(npair, 1, LANES)
    return bblk.astype(BF16), ar, ai, cblk.astype(BF16)


def kernel(x, mix_norm_g, ffn_norm_g, final_norm_g, ev_w_in, ev_w_out, s5_lam_re, s5_lam_im,
           s5_log_dt, s5_b_re, s5_b_im, s5_c_re, s5_c_im, s5_d, s5_w_glu, s5_b_glu, gm_w_s,
           gm_b_s, gm_v_g, od_w_in, od_conv_w, od_conv_b, od_w_out, ffn_w_up, ffn_conv_w,
           ffn_conv_b, ffn_w_down):
    depth = mix_norm_g.shape[0]
    row = lambda a: a.reshape(1, -1).astype(F32)

    def col_tiles(w, parts, tf):
        k, n = w.shape[0], w.shape[1] // parts
        return w.reshape(k, parts, n // tf, tf).transpose(2, 0, 1, 3).reshape(n // tf, k, parts * tf)

    def conv_tiles(cw, cb, parts, tf):
        t = col_tiles(jnp.concatenate([cw, cb[None, :]], axis=0).astype(F32), parts, tf)
        return jnp.broadcast_to(t[:, :, None, :], t.shape[:2] + (SUBLANES, t.shape[2]))

    h = x
    for layer in range(depth):
        if layer % 2 == 0:
            e = layer // 2
            bblk, ar, ai, cblk = _s5_operands(s5_lam_re[e], s5_lam_im[e], s5_log_dt[e],
                                              s5_b_re[e], s5_b_im[e], s5_c_re[e], s5_c_im[e])
            h = _mix0_call(h, row(mix_norm_g[layer]), ev_w_in[e].astype(BF16), bblk, ar, ai, cblk,
                           row(s5_d[e]), s5_w_glu[e].astype(BF16), row(s5_b_glu[e]),
                           gm_w_s[e], gm_b_s[e][:, :, None], row(gm_v_g[e]),
                           ev_w_out[e].astype(BF16))
        else:
            o = layer // 2
            h = _shortconv_call(h, row(mix_norm_g[layer]),
                                col_tiles(od_w_in[o].astype(BF16), 3, MIX_TILE),
                                conv_tiles(od_conv_w[o], od_conv_b[o], 1, MIX_TILE),
                                od_w_out[o].astype(BF16))
        gf = row(final_norm_g) if layer == depth - 1 else None
        h = _ffn_call(h, row(ffn_norm_g[layer]),
                      col_tiles(ffn_w_up[layer].astype(BF16), 2, FFN_TILE),
                      conv_tiles(ffn_conv_w[layer], ffn_conv_b[layer], 2, FFN_TILE),
                      ffn_w_down[layer].astype(BF16), gf)
    return h
```

```python
import functools
import math

import jax
import jax.numpy as jnp
from jax import lax
from jax.experimental import pallas as pl
from jax.experimental.pallas import tpu as pltpu

F32 = jnp.float32
BF16 = jnp.bfloat16

EPS = 1e-6
LAMBDA_RE_MAX = -1e-4
LANES = 128
SUBLANES = 8
HALO = 2 * SUBLANES
SUB_ROWS = SUBLANES * SUBLANES
WORK_SUB = HALO + SUB_ROWS
ROW_CHUNK = 32
FFN_TILE = 256
FFN_GROUP = 4
MIX_TILE = 256
MIX_GROUP = 2
SSM_GROUP = 16
SSM_STATE = 64
GMLP_HEAD = 128
CHUNK = 128
VMEM_LIMIT = 56 * 1024 * 1024


def _rms(x, g):
    ms = jnp.mean(x * x, axis=-1, keepdims=True)
    return x * lax.rsqrt(ms + EPS) * g


def _gelu(x):
    c = math.sqrt(2.0 / math.pi)
    return 0.5 * x * (1.0 + jnp.tanh(c * (x + 0.044715 * (x * x * x))))


def _sigmoid(x):
    return 1.0 / (1.0 + jnp.exp(-x))


def _const_spec(shape):
    nd = len(shape)
    return pl.BlockSpec(shape, lambda *_: (0,) * nd, pipeline_mode=pl.Buffered(1))


def _transposed_tile(slab_ref, sb, k):
    return slab_ref[pl.ds(sb * SUB_ROWS + k, SUBLANES, stride=SUBLANES), :]


def _load_rows_permuted(x_refs, g, xn_ref):
    for t in range(0, xn_ref.shape[0] // SUBLANES, 2):
        xp = jnp.concatenate(
            [jnp.concatenate([_transposed_tile(xr, tt // SUBLANES, tt % SUBLANES) for xr in x_refs],
                             axis=-1) for tt in (t, t + 1)], axis=0)
        xn_ref[t * SUBLANES:(t + 2) * SUBLANES, :] = _rms(xp, g).astype(BF16)


def _store_with_halo(work, carry_ref, col0, hp):
    tm, w = hp.shape
    sub = lax.broadcasted_iota(jnp.int32, (SUBLANES, w), 0)
    for sb in range(tm // SUB_ROWS):
        base = sb * WORK_SUB
        work[base + HALO:base + WORK_SUB, :] = hp[sb * SUB_ROWS:(sb + 1) * SUB_ROWS, :]
        for k in range(2):
            tail = work[base + WORK_SUB - HALO + k * SUBLANES:base + WORK_SUB - HALO + (k + 1) * SUBLANES, :]
            if sb == 0:
                prev = carry_ref[k * SUBLANES:(k + 1) * SUBLANES, col0:col0 + w]
            else:
                prev = work[base - HALO + k * SUBLANES:base - HALO + (k + 1) * SUBLANES, :]
            work[base + k * SUBLANES:base + (k + 1) * SUBLANES, :] = pltpu.roll(
                jnp.where(sub == SUBLANES - 1, prev, tail), 1, 0)
    last = (tm // SUB_ROWS) * WORK_SUB
    carry_ref[:, col0:col0 + w] = work[last - HALO:last, :]


def _conv_rows(work, cw, sb, r, n):
    s = SUBLANES
    r0 = sb * WORK_SUB + r
    tiles = lambda a: a.reshape(n // s, s, a.shape[-1])
    c = (cw[3] + cw[0] * tiles(work[r0:r0 + n, :]) + cw[1] * tiles(work[s + r0:s + r0 + n, :])
         + cw[2] * tiles(work[2 * s + r0:2 * s + r0 + n, :]))
    return c.reshape(n, c.shape[-1])


def _acc_slabs(acc_ref, dd, first):
    for c in range(acc_ref.shape[0]):
        if first:
            acc_ref[c] = dd[:, c * LANES:(c + 1) * LANES]
        else:
            acc_ref[c] += dd[:, c * LANES:(c + 1) * LANES]


def _natural_tile(acc_ref, t):
    return jnp.concatenate([_transposed_tile(acc_ref.at[c], t // SUBLANES, t % SUBLANES)
                            for c in range(acc_ref.shape[0])], axis=-1)


def _run_tiles(nt, dk, up, gate, down):
    up(0)
    for j in range(nt):
        if j + 1 < nt:
            up(j + 1)
        if j > 0 and j % dk == 0:
            down(j // dk - 1)
        gate(j)
    down((nt - 1) // dk)


def _down_group(a_ref, w_ref, acc_ref, g, nt, dk, tf):
    n = min(dk, nt - g * dk)
    dd = jnp.dot(a_ref[g % 2, :, 0:n * tf], w_ref[g * dk * tf:(g * dk + n) * tf, :],
                 preferred_element_type=F32)
    _acc_slabs(acc_ref, dd, first=(g == 0))


def _ffn_kernel(*refs, tm, tf, nt, dk, d, final):
    ns = d // LANES
    x_refs, refs = refs[:ns], refs[ns:]
    if final:
        (g_ref, wup_ref, cw_ref, wdn_ref, gf_ref, o_ref,
         xn_ref, carry_ref, work_ref, a_ref, acc_ref) = refs
    else:
        (g_ref, wup_ref, cw_ref, wdn_ref, o_ref,
         xn_ref, carry_ref, work_ref, a_ref, acc_ref) = refs

    @pl.when(pl.program_id(1) == 0)
    def _():
        carry_ref[...] = jnp.zeros_like(carry_ref)

    _load_rows_permuted(x_refs, g_ref[...], xn_ref)

    def up(j):
        hp = jnp.dot(xn_ref[...], wup_ref[j], preferred_element_type=F32)
        _store_with_halo(work_ref.at[j % 2], carry_ref, j * 2 * tf, hp)

    def gate(j):
        work, cw = work_ref.at[j % 2], cw_ref.at[j]
        for r in range(0, tm, ROW_CHUNK):
            c = _conv_rows(work, cw, r // SUB_ROWS, r % SUB_ROWS, ROW_CHUNK)
            gt, vl = c[:, :tf], c[:, tf:]
            a_ref[(j // dk) % 2, r:r + ROW_CHUNK, (j % dk) * tf:(j % dk + 1) * tf] = (
                gt * _sigmoid(gt) * vl).astype(BF16)

    down = functools.partial(_down_group, a_ref, wdn_ref, acc_ref, nt=nt, dk=dk, tf=tf)
    _run_tiles(nt, dk, up, gate, down)

    for t in range(tm // SUBLANES):
        rows = slice(t * SUBLANES, (t + 1) * SUBLANES)
        out = jnp.concatenate([xr[rows, :] for xr in x_refs], axis=-1) + _natural_tile(acc_ref, t)
        if final:
            out = _rms(out, gf_ref[...])
        o_ref[rows, :] = out


def _slab_specs(tm, d):
    return [pl.BlockSpec((None, tm, LANES), lambda bi, ti, c=c: (bi, ti, c)) for c in range(d // LANES)]


def _ffn_call(h, g, wup, cw, wdn, gf=None, *, tm=512, dk=FFN_GROUP):
    b, s, d = h.shape
    tm = min(tm, s)
    nt, tf = wup.shape[0], wup.shape[2] // 2
    ns = d // LANES
    final = gf is not None
    in_specs = _slab_specs(tm, d) + [_const_spec((1, d)), _const_spec(wup.shape),
                                     _const_spec(cw.shape), _const_spec(wdn.shape)]
    args = [h] * ns + [g, wup, cw, wdn]
    if final:
        in_specs.append(_const_spec((1, d)))
        args.append(gf)
    return pl.pallas_call(
        functools.partial(_ffn_kernel, tm=tm, tf=tf, nt=nt, dk=dk, d=d, final=final),
        grid=(b, s // tm),
        in_specs=in_specs,
        out_specs=pl.BlockSpec((None, tm, d), lambda bi, ti: (bi, ti, 0)),
        out_shape=jax.ShapeDtypeStruct((b, s, d), F32),
        scratch_shapes=[
            pltpu.VMEM((tm, d), BF16),
            pltpu.VMEM((HALO, nt * 2 * tf), F32),
            pltpu.VMEM((2, tm // SUB_ROWS * WORK_SUB, 2 * tf), F32),
            pltpu.VMEM((2, tm, dk * tf), BF16),
            pltpu.VMEM((ns, tm, LANES), F32),
        ],
        compiler_params=pltpu.CompilerParams(
            dimension_semantics=("arbitrary", "arbitrary"),
            vmem_limit_bytes=VMEM_LIMIT),
        name="conv_ffn_final" if final else "conv_ffn",
    )(*args)


def _shortconv_kernel(*refs, tm, tf, nt, dk, d):
    ns = d // LANES
    x_refs, refs = refs[:ns], refs[ns:]
    (g_ref, win_ref, cw_ref, wout_ref, o_ref,
     xn_ref, carry_ref, work_ref, bg_ref, a_ref, acc_ref) = refs

    @pl.when(pl.program_id(1) == 0)
    def _():
        carry_ref[...] = jnp.zeros_like(carry_ref)

    _load_rows_permuted(x_refs, g_ref[...], xn_ref)

    def up(j):
        hp = jnp.dot(xn_ref[...], win_ref[j], preferred_element_type=F32)
        bg_ref[j % 2] = hp[:, :tf]
        _store_with_halo(work_ref.at[j % 2], carry_ref, j * tf, hp[:, tf:2 * tf] * hp[:, 2 * tf:])

    def gate(j):
        work, cw = work_ref.at[j % 2], cw_ref.at[j]
        for r in range(0, tm, ROW_CHUNK):
            c = _conv_rows(work, cw, r // SUB_ROWS, r % SUB_ROWS, ROW_CHUNK)
            a_ref[(j // dk) % 2, r:r + ROW_CHUNK, (j % dk) * tf:(j % dk + 1) * tf] = (
                bg_ref[j % 2, r:r + ROW_CHUNK, :] * c).astype(BF16)

    down = functools.partial(_down_group, a_ref, wout_ref, acc_ref, nt=nt, dk=dk, tf=tf)
    _run_tiles(nt, dk, up, gate, down)

    for t in range(tm // SUBLANES):
        rows = slice(t * SUBLANES, (t + 1) * SUBLANES)
        o_ref[rows, :] = (jnp.concatenate([xr[rows, :] for xr in x_refs], axis=-1)
                          + _natural_tile(acc_ref, t))


def _shortconv_call(h, g, win, cw, wout, *, tm=512, dk=MIX_GROUP):
    b, s, d = h.shape
    tm = min(tm, s)
    nt, tf = win.shape[0], win.shape[2] // 3
    ns = d // LANES
    return pl.pallas_call(
        functools.partial(_shortconv_kernel, tm=tm, tf=tf, nt=nt, dk=dk, d=d),
        grid=(b, s // tm),
        in_specs=_slab_specs(tm, d) + [_const_spec((1, d)), _const_spec(win.shape),
                                       _const_spec(cw.shape), _const_spec(wout.shape)],
        out_specs=pl.BlockSpec((None, tm, d), lambda bi, ti: (bi, ti, 0)),
        out_shape=jax.ShapeDtypeStruct((b, s, d), F32),
        scratch_shapes=[
            pltpu.VMEM((tm, d), BF16),
            pltpu.VMEM((HALO, nt * tf), F32),
            pltpu.VMEM((2, tm // SUB_ROWS * WORK_SUB, tf), F32),
            pltpu.VMEM((2, tm, tf), F32),
            pltpu.VMEM((2, tm, dk * tf), BF16),
            pltpu.VMEM((ns, tm, LANES), F32),
        ],
        compiler_params=pltpu.CompilerParams(
            dimension_semantics=("arbitrary", "arbitrary"),
            vmem_limit_bytes=VMEM_LIMIT),
        name="shortconv_mixer",
    )(*([h] * ns), g, win, cw, wout)


def _s5_prep_kernel(lr_ref, li_ref, ldt_ref, br_ref, bi_ref, ar_ref, ai_ref, bbr_ref, bbi_ref):
    lr = jnp.minimum(lr_ref[...], LAMBDA_RE_MAX)
    li = li_ref[...]
    dt = jnp.exp(ldt_ref[...])
    mag = jnp.exp(lr * dt)
    ab_re = mag * jnp.cos(li * dt)
    ab_im = mag * jnp.sin(li * dt)
    den = lr * lr + li * li
    nr = ab_re - 1.0
    ni = ab_im
    z_re = (nr * lr + ni * li) / den
    z_im = (ni * lr - nr * li) / den
    br = br_ref[...]
    bi = bi_ref[...]
    ar_ref[...] = ab_re
    ai_ref[...] = ab_im
    bbr_ref[...] = z_re * br - z_im * bi
    bbi_ref[...] = z_re * bi + z_im * br


def _s5_prep(lam_re, lam_im, log_dt, b_re, b_im):
    g, p = lam_re.shape
    n = g * p
    col = lambda a: a.reshape(n, 1)
    ldt = jnp.broadcast_to(log_dt[:, None], (g, p))
    outs = pl.pallas_call(
        _s5_prep_kernel,
        out_shape=[jax.ShapeDtypeStruct((n, 1), F32), jax.ShapeDtypeStruct((n, 1), F32),
                   jax.ShapeDtypeStruct((n, SSM_GROUP), F32),
                   jax.ShapeDtypeStruct((n, SSM_GROUP), F32)],
        name="s5_discretise",
    )(col(lam_re), col(lam_im), col(ldt), b_re.reshape(n, SSM_GROUP), b_im.reshape(n, SSM_GROUP))
    return outs


def _mix0_kernel(x_ref, g_ref, win_ref, bblk_ref, ar_ref, ai_ref, cblk_ref, dsk_ref,
                 wglu_ref, bglu_ref, ws_ref, bs_ref, gv_ref, wout_ref, o_ref,
                 xn_ref, up_ref, utb_ref, xs_ref, hst_ref, ytb_ref, vn_ref, gm_ref, s5_ref, acc_ref,
                 *, nb, tc, pitch, d, ssm_w, gm_w):
    rows = nb * tc
    nslab = xs_ref.shape[0]
    npair = nslab // 2
    uslab = ssm_w // LANES
    heads = gm_w // GMLP_HEAD

    @pl.when(pl.program_id(1) == 0)
    def _():
        hst_ref[...] = jnp.zeros_like(hst_ref)

    x = x_ref[...].reshape(rows, d)
    xn_ref[...] = _rms(x, g_ref[...]).astype(BF16)

    half = rows // 2
    for r0 in (0, half):
        u = jnp.dot(xn_ref[r0:r0 + half, :], win_ref[:, 0:ssm_w], preferred_element_type=F32)
        for b in range(r0 // tc, (r0 + half) // tc):
            for c in range(uslab):
                up_ref[c, b * pitch:b * pitch + tc, :] = u[b * tc - r0:(b + 1) * tc - r0,
                                                           c * LANES:(c + 1) * LANES]
    for t in range(tc):
        utb_ref[t * nb:(t + 1) * nb, :] = jnp.concatenate(
            [up_ref[c, pl.ds(t, nb, stride=pitch), :] for c in range(uslab)], axis=-1)

    ub = utb_ref[...].astype(BF16)
    for c in range(npair):
        xs = jnp.dot(ub, bblk_ref[:, c * 2 * LANES:(c + 1) * 2 * LANES], preferred_element_type=F32)
        xs_ref[2 * c] = xs[:, 0:LANES]
        xs_ref[2 * c + 1] = xs[:, LANES:2 * LANES]

    v0 = ssm_w + gm_w
    vv = _gelu(jnp.dot(xn_ref[...], win_ref[:, v0:v0 + gm_w], preferred_element_type=F32))
    vn_ref[...] = _rms(vv, gv_ref[...]).astype(BF16)

    ar = [jnp.broadcast_to(ar_ref[c], (nb, LANES)) for c in range(npair)]
    ai = [jnp.broadcast_to(ai_ref[c], (nb, LANES)) for c in range(npair)]
    h = [hst_ref[k] for k in range(nslab)]
    for t in range(tc):
        rt = slice(t * nb, (t + 1) * nb)
        for c in range(npair):
            hr, hi = h[2 * c], h[2 * c + 1]
            nr = ar[c] * hr - ai[c] * hi + xs_ref[2 * c, rt, :]
            ni = ar[c] * hi + ai[c] * hr + xs_ref[2 * c + 1, rt, :]
            xs_ref[2 * c, rt, :] = nr
            xs_ref[2 * c + 1, rt, :] = ni
            h[2 * c], h[2 * c + 1] = nr, ni
    for k in range(nslab):
        hst_ref[k] = h[k]

    rr = lax.broadcasted_iota(jnp.int32, (CHUNK, CHUNK), 0)
    cc = lax.broadcasted_iota(jnp.int32, (CHUNK, CHUNK), 1)
    tril = rr >= cc
    blocks = [b * tc + n * CHUNK for b in range(nb) for n in range(tc // CHUNK)]
    for hp in range(heads // 2):
        c0 = ssm_w + hp * 2 * GMLP_HEAD
        uu = _gelu(jnp.dot(xn_ref[...], win_ref[:, c0:c0 + 2 * GMLP_HEAD],
                           preferred_element_type=F32))
        for k in range(2):
            hh = 2 * hp + k
            w = jnp.where(tril, ws_ref[hh], 0.0).astype(BF16)
            vcols = slice(hh * GMLP_HEAD, (hh + 1) * GMLP_HEAD)
            rhs = jnp.concatenate([vn_ref[r0:r0 + CHUNK, vcols] for r0 in blocks], axis=1)
            gate = jnp.dot(w, rhs, preferred_element_type=F32) + bs_ref[hh]
            for i, r0 in enumerate(blocks):
                gm_ref[r0:r0 + CHUNK, hh * GMLP_HEAD:(hh + 1) * GMLP_HEAD] = (
                    uu[r0:r0 + CHUNK, k * GMLP_HEAD:(k + 1) * GMLP_HEAD]
                    * gate[:, i * GMLP_HEAD:(i + 1) * GMLP_HEAD]).astype(BF16)

    acc_ref[...] = x + jnp.dot(gm_ref[...], wout_ref[ssm_w:, :], preferred_element_type=F32)

    for r0 in (0, half):
        rs = slice(r0, r0 + half)
        hb = jnp.concatenate([xs_ref[k, rs, :] for k in range(nslab)], axis=-1).astype(BF16)
        y = jnp.dot(hb, cblk_ref[...], preferred_element_type=F32)
        y = _gelu(y + dsk_ref[...] * utb_ref[rs, :])
        gl = jnp.dot(y.astype(BF16), wglu_ref[...], preferred_element_type=F32) + bglu_ref[...]
        y = y * _sigmoid(gl)
        for c in range(uslab):
            ytb_ref[c, rs, :] = y[:, c * LANES:(c + 1) * LANES]
    for b in range(nb):
        for t0 in range(0, tc, 2 * SUBLANES):
            tile = jnp.concatenate(
                [jnp.concatenate([ytb_ref[c, pl.ds((t0 + dt) * nb + b, SUBLANES, stride=nb), :]
                                  for c in range(uslab)], axis=-1) for dt in (0, SUBLANES)], axis=0)
            s5_ref[b * tc + t0:b * tc + t0 + 2 * SUBLANES, :] = tile.astype(BF16)

    out = acc_ref[...] + jnp.dot(s5_ref[...], wout_ref[0:ssm_w, :], preferred_element_type=F32)
    o_ref[...] = out.reshape(nb, tc, d)


def _mix0_call(h, g, win, bblk, ar, ai, cblk, dsk, wglu, bglu, ws, bs, gv, wout, *, nb=8, tc=128):
    b, s, d = h.shape
    ssm_w = wglu.shape[0]
    gm_w = gv.shape[1]
    nslab = bblk.shape[1] // LANES
    pitch = tc + SUBLANES
    rows = nb * tc
    kern = functools.partial(_mix0_kernel, nb=nb, tc=tc, pitch=pitch, d=d, ssm_w=ssm_w, gm_w=gm_w)
    return pl.pallas_call(
        kern,
        grid=(b // nb, s // tc),
        in_specs=[
            pl.BlockSpec((nb, tc, d), lambda bi, ti: (bi, ti, 0)),
            _const_spec((1, d)),
            _const_spec(win.shape),
            _const_spec(bblk.shape),
            _const_spec(ar.shape),
            _const_spec(ai.shape),
            _const_spec(cblk.shape),
            _const_spec((1, ssm_w)),
            _const_spec(wglu.shape),
            _const_spec((1, ssm_w)),
            _const_spec(ws.shape),
            _const_spec(bs.shape),
            _const_spec((1, gm_w)),
            _const_spec(wout.shape),
        ],
        out_specs=pl.BlockSpec((nb, tc, d), lambda bi, ti: (bi, ti, 0)),
        out_shape=jax.ShapeDtypeStruct((b, s, d), F32),
        scratch_shapes=[
            pltpu.VMEM((rows, d), BF16),
            pltpu.VMEM((ssm_w // LANES, nb * pitch, LANES), F32),
            pltpu.VMEM((rows, ssm_w), F32),
            pltpu.VMEM((nslab, rows, LANES), F32),
            pltpu.VMEM((nslab, nb, LANES), F32),
            pltpu.VMEM((ssm_w // LANES, rows, LANES), F32),
            pltpu.VMEM((rows, gm_w), BF16),
            pltpu.VMEM((rows, gm_w), BF16),
            pltpu.VMEM((rows, ssm_w), BF16),
            pltpu.VMEM((rows, d), F32),
        ],
        compiler_params=pltpu.CompilerParams(
            dimension_semantics=("arbitrary", "arbitrary"),
            vmem_limit_bytes=VMEM_LIMIT),
        name="s5_gmlp_mixer",
    )(h, g, win, bblk, ar, ai, cblk, dsk, wglu, bglu, ws, bs, gv, wout)


def _s5_operands(lam_re, lam_im, log_dt, b_re, b_im, c_re, c_im):
    g, p = lam_re.shape
    hdim = b_re.shape[-1]
    n = g * p
    npair = n // LANES
    ab_re, ab_im, bb_re, bb_im = _s5_prep(lam_re, lam_im, log_dt, b_re, b_im)
    eye = jnp.eye(g, dtype=F32)

    def b_dense(bb):
        bb = bb.reshape(g, p, hdim)
        return jnp.einsum('gph,gk->ghkp', bb, eye).reshape(g * hdim, npair, LANES)

    def c_dense(cm):
        return jnp.einsum('ghp,gk->gpkh', cm, eye).reshape(npair, LANES, g * hdim)

    bblk = jnp.stack([b_dense(bb_re), b_dense(bb_im)], axis=2).reshape(g * hdim, 2 * n)
    cblk = jnp.stack([c_dense(c_re), -c_dense(c_im)], axis=1).reshape(2 * n, g * hdim)
    ar = ab_re.reshape(npair, 1, LANES)
    ai = ab_im.reshape(npair, 1, LANES)
    return bblk.astype(BF16), ar, ai, cblk.astype(BF16)


def kernel(x, mix_norm_g, ffn_norm_g, final_norm_g, ev_w_in, ev_w_out, s5_lam_re, s5_lam_im,
           s5_log_dt, s5_b_re, s5_b_im, s5_c_re, s5_c_im, s5_d, s5_w_glu, s5_b_glu, gm_w_s,
           gm_b_s, gm_v_g, od_w_in, od_conv_w, od_conv_b, od_w_out, ffn_w_up, ffn_conv_w,
           ffn_conv_b, ffn_w_down):
    depth = mix_norm_g.shape[0]
    row = lambda a: a.reshape(1, -1).astype(F32)

    def col_tiles(w, parts, tf):
        k, n = w.shape[0], w.shape[1] // parts
        return w.reshape(k, parts, n // tf, tf).transpose(2, 0, 1, 3).reshape(n // tf, k, parts * tf)

    def conv_tiles(cw, cb, parts, tf):
        t = col_tiles(jnp.concatenate([cw, cb[None, :]], axis=0).astype(F32), parts, tf)
        return jnp.broadcast_to(t[:, :, None, :], t.shape[:2] + (SUBLANES, t.shape[2]))

    h = x
    for layer in range(depth):
        if layer % 2 == 0:
            e = layer // 2
            bblk, ar, ai, cblk = _s5_operands(s5_lam_re[e], s5_lam_im[e], s5_log_dt[e],
                                              s5_b_re[e], s5_b_im[e], s5_c_re[e], s5_c_im[e])
            h = _mix0_call(h, row(mix_norm_g[layer]), ev_w_in[e].astype(BF16), bblk, ar, ai, cblk,
                           row(s5_d[e]), s5_w_glu[e].astype(BF16), row(s5_b_glu[e]),
                           gm_w_s[e], gm_b_s[e][:, :, None], row(gm_v_g[e]),
                           ev_w_out[e].astype(BF16))
        else:
            o = layer // 2
            h = _shortconv_call(h, row(mix_norm_g[layer]),
                                col_tiles(od_w_in[o].astype(BF16), 3, MIX_TILE),
                                conv_tiles(od_conv_w[o], od_conv_b[o], 1, MIX_TILE),
                                od_w_out[o].astype(BF16))
        gf = row(final_norm_g) if layer == depth - 1 else None
        h = _ffn_call(h, row(ffn_norm_g[layer]),
                      col_tiles(ffn_w_up[layer].astype(BF16), 2, FFN_TILE),
                      conv_tiles(ffn_conv_w[layer], ffn_conv_b[layer], 2, FFN_TILE),
                      ffn_w_down[layer].astype(BF16), gf)
    return h
```

```python
import functools
import math

import jax
import jax.numpy as jnp
from jax import lax
from jax.experimental import pallas as pl
from jax.experimental.pallas import tpu as pltpu

F32 = jnp.float32
BF16 = jnp.bfloat16

EPS = 1e-6
LAMBDA_RE_MAX = -1e-4
LANES = 128
SUBLANES = 8
HALO = 2 * SUBLANES
SUB_ROWS = SUBLANES * SUBLANES
WORK_SUB = HALO + SUB_ROWS
ROW_CHUNK = 32
FFN_TILE = 256
FFN_GROUP = 4
MIX_TILE = 256
MIX_GROUP = 2
SSM_GROUP = 16
SSM_STATE = 64
GMLP_HEAD = 128
CHUNK = 128
VMEM_LIMIT = 56 * 1024 * 1024


def _rms(x, g):
    ms = jnp.mean(x * x, axis=-1, keepdims=True)
    return x * lax.rsqrt(ms + EPS) * g


def _gelu(x):
    c = math.sqrt(2.0 / math.pi)
    return 0.5 * x * (1.0 + jnp.tanh(c * (x + 0.044715 * (x * x * x))))


def _sigmoid(x):
    return 1.0 / (1.0 + jnp.exp(-x))


def _const_spec(shape):
    nd = len(shape)
    return pl.BlockSpec(shape, lambda *_: (0,) * nd, pipeline_mode=pl.Buffered(1))


def _transposed_tile(slab_ref, sb, k):
    return slab_ref[pl.ds(sb * SUB_ROWS + k, SUBLANES, stride=SUBLANES), :]


def _load_rows_permuted(x_refs, g, xn_ref):
    for t in range(0, xn_ref.shape[0] // SUBLANES, 2):
        xp = jnp.concatenate(
            [jnp.concatenate([_transposed_tile(xr, tt // SUBLANES, tt % SUBLANES) for xr in x_refs],
                             axis=-1) for tt in (t, t + 1)], axis=0)
        xn_ref[t * SUBLANES:(t + 2) * SUBLANES, :] = _rms(xp, g).astype(BF16)


def _store_with_halo(work, carry_ref, col0, hp):
    tm, w = hp.shape
    sub = lax.broadcasted_iota(jnp.int32, (SUBLANES, w), 0)
    for sb in range(tm // SUB_ROWS):
        base = sb * WORK_SUB
        work[base + HALO:base + WORK_SUB, :] = hp[sb * SUB_ROWS:(sb + 1) * SUB_ROWS, :]
        for k in range(2):
            tail = work[base + WORK_SUB - HALO + k * SUBLANES:base + WORK_SUB - HALO + (k + 1) * SUBLANES, :]
            if sb == 0:
                prev = carry_ref[k * SUBLANES:(k + 1) * SUBLANES, col0:col0 + w]
            else:
                prev = work[base - HALO + k * SUBLANES:base - HALO + (k + 1) * SUBLANES, :]
            work[base + k * SUBLANES:base + (k + 1) * SUBLANES, :] = pltpu.roll(
                jnp.where(sub == SUBLANES - 1, prev, tail), 1, 0)
    last = (tm // SUB_ROWS) * WORK_SUB
    carry_ref[:, col0:col0 + w] = work[last - HALO:last, :]


def _conv_rows(work, cw, sb, r, n):
    s = SUBLANES
    r0 = sb * WORK_SUB + r
    tiles = lambda a: a.reshape(n // s, s, a.shape[-1])
    c = (cw[3] + cw[0] * tiles(work[r0:r0 + n, :]) + cw[1] * tiles(work[s + r0:s + r0 + n, :])
         + cw[2] * tiles(work[2 * s + r0:2 * s + r0 + n, :]))
    return c.reshape(n, c.shape[-1])


def _acc_slabs(acc_ref, dd, first):
    for c in range(acc_ref.shape[0]):
        if first:
            acc_ref[c] = dd[:, c * LANES:(c + 1) * LANES]
        else:
            acc_ref[c] += dd[:, c * LANES:(c + 1) * LANES]


def _natural_tile(acc_ref, t):
    return jnp.concatenate([_transposed_tile(acc_ref.at[c], t // SUBLANES, t % SUBLANES)
                            for c in range(acc_ref.shape[0])], axis=-1)


def _run_tiles(nt, dk, up, gate, down):
    up(0)
    for j in range(nt):
        if j + 1 < nt:
            up(j + 1)
        if j > 0 and j % dk == 0:
            down(j // dk - 1)
        gate(j)
    down((nt - 1) // dk)


def _down_group(a_ref, w_ref, acc_ref, g, nt, dk, tf):
    n = min(dk, nt - g * dk)
    dd = jnp.dot(a_ref[g % 2, :, 0:n * tf], w_ref[g * dk * tf:(g * dk + n) * tf, :],
                 preferred_element_type=F32)
    _acc_slabs(acc_ref, dd, first=(g == 0))


def _ffn_kernel(*refs, tm, tf, nt, dk, d, final):
    ns = d // LANES
    x_refs, refs = refs[:ns], refs[ns:]
    if final:
        (g_ref, wup_ref, cw_ref, wdn_ref, gf_ref, o_ref,
         xn_ref, carry_ref, work_ref, a_ref, acc_ref) = refs
    else:
        (g_ref, wup_ref, cw_ref, wdn_ref, o_ref,
         xn_ref, carry_ref, work_ref, a_ref, acc_ref) = refs

    @pl.when(pl.program_id(1) == 0)
    def _():
        carry_ref[...] = jnp.zeros_like(carry_ref)

    _load_rows_permuted(x_refs, g_ref[...], xn_ref)

    def up(j):
        dff = nt * tf
        for part in range(2):
            c0 = part * dff + j * tf
            hp = jnp.dot(xn_ref[...], wup_ref[:, c0:c0 + tf], preferred_element_type=F32)
            _store_with_halo(work_ref.at[j % 2, :, part * tf:(part + 1) * tf], carry_ref, c0, hp)

    def gate(j):
        work, cw = work_ref.at[j % 2], cw_ref.at[j]
        for r in range(0, tm, ROW_CHUNK):
            c = _conv_rows(work, cw, r // SUB_ROWS, r % SUB_ROWS, ROW_CHUNK)
            gt, vl = c[:, :tf], c[:, tf:]
            a_ref[(j // dk) % 2, r:r + ROW_CHUNK, (j % dk) * tf:(j % dk + 1) * tf] = (
                gt * _sigmoid(gt) * vl).astype(BF16)

    down = functools.partial(_down_group, a_ref, wdn_ref, acc_ref, nt=nt, dk=dk, tf=tf)
    _run_tiles(nt, dk, up, gate, down)

    for t in range(tm // SUBLANES):
        rows = slice(t * SUBLANES, (t + 1) * SUBLANES)
        out = jnp.concatenate([xr[rows, :] for xr in x_refs], axis=-1) + _natural_tile(acc_ref, t)
        if final:
            out = _rms(out, gf_ref[...])
        o_ref[rows, :] = out


def _slab_specs(tm, d):
    return [pl.BlockSpec((None, tm, LANES), lambda bi, ti, c=c: (bi, ti, c)) for c in range(d // LANES)]


def _ffn_call(h, g, wup, cw, wdn, gf=None, *, tm=512, dk=FFN_GROUP):
    b, s, d = h.shape
    tm = min(tm, s)
    nt, tf = cw.shape[0], cw.shape[3] // 2
    ns = d // LANES
    final = gf is not None
    in_specs = _slab_specs(tm, d) + [_const_spec((1, d)), _const_spec(wup.shape),
                                     _const_spec(cw.shape), _const_spec(wdn.shape)]
    args = [h] * ns + [g, wup, cw, wdn]
    if final:
        in_specs.append(_const_spec((1, d)))
        args.append(gf)
    return pl.pallas_call(
        functools.partial(_ffn_kernel, tm=tm, tf=tf, nt=nt, dk=dk, d=d, final=final),
        grid=(b, s // tm),
        in_specs=in_specs,
        out_specs=pl.BlockSpec((None, tm, d), lambda bi, ti: (bi, ti, 0)),
        out_shape=jax.ShapeDtypeStruct((b, s, d), F32),
        scratch_shapes=[
            pltpu.VMEM((tm, d), BF16),
            pltpu.VMEM((HALO, nt * 2 * tf), F32),
            pltpu.VMEM((2, tm // SUB_ROWS * WORK_SUB, 2 * tf), F32),
            pltpu.VMEM((2, tm, dk * tf), BF16),
            pltpu.VMEM((ns, tm, LANES), F32),
        ],
        compiler_params=pltpu.CompilerParams(
            dimension_semantics=("arbitrary", "arbitrary"),
            vmem_limit_bytes=VMEM_LIMIT),
        name="conv_ffn_final" if final else "conv_ffn",
    )(*args)


def _shortconv_kernel(*refs, tm, tf, nt, dk, d):
    ns = d // LANES
    x_refs, refs = refs[:ns], refs[ns:]
    (g_ref, win_ref, cw_ref, wout_ref, o_ref,
     xn_ref, carry_ref, work_ref, bg_ref, a_ref, acc_ref) = refs

    @pl.when(pl.program_id(1) == 0)
    def _():
        carry_ref[...] = jnp.zeros_like(carry_ref)

    _load_rows_permuted(x_refs, g_ref[...], xn_ref)

    def up(j):
        bg, cg, hx = [jnp.dot(xn_ref[...], win_ref[:, part * d + j * tf:part * d + (j + 1) * tf],
                              preferred_element_type=F32) for part in range(3)]
        bg_ref[j % 2] = bg
        _store_with_halo(work_ref.at[j % 2], carry_ref, j * tf, cg * hx)

    def gate(j):
        work, cw = work_ref.at[j % 2], cw_ref.at[j]
        for r in range(0, tm, ROW_CHUNK):
            c = _conv_rows(work, cw, r // SUB_ROWS, r % SUB_ROWS, ROW_CHUNK)
            a_ref[(j // dk) % 2, r:r + ROW_CHUNK, (j % dk) * tf:(j % dk + 1) * tf] = (
                bg_ref[j % 2, r:r + ROW_CHUNK, :] * c).astype(BF16)

    down = functools.partial(_down_group, a_ref, wout_ref, acc_ref, nt=nt, dk=dk, tf=tf)
    _run_tiles(nt, dk, up, gate, down)

    for t in range(tm // SUBLANES):
        rows = slice(t * SUBLANES, (t + 1) * SUBLANES)
        o_ref[rows, :] = (jnp.concatenate([xr[rows, :] for xr in x_refs], axis=-1)
                          + _natural_tile(acc_ref, t))


def _shortconv_call(h, g, win, cw, wout, *, tm=512, dk=MIX_GROUP):
    b, s, d = h.shape
    tm = min(tm, s)
    nt, tf = cw.shape[0], cw.shape[3]
    ns = d // LANES
    return pl.pallas_call(
        functools.partial(_shortconv_kernel, tm=tm, tf=tf, nt=nt, dk=dk, d=d),
        grid=(b, s // tm),
        in_specs=_slab_specs(tm, d) + [_const_spec((1, d)), _const_spec(win.shape),
                                       _const_spec(cw.shape), _const_spec(wout.shape)],
        out_specs=pl.BlockSpec((None, tm, d), lambda bi, ti: (bi, ti, 0)),
        out_shape=jax.ShapeDtypeStruct((b, s, d), F32),
        scratch_shapes=[
            pltpu.VMEM((tm, d), BF16),
            pltpu.VMEM((HALO, nt * tf), F32),
            pltpu.VMEM((2, tm // SUB_ROWS * WORK_SUB, tf), F32),
            pltpu.VMEM((2, tm, tf), F32),
            pltpu.VMEM((2, tm, dk * tf), BF16),
            pltpu.VMEM((ns, tm, LANES), F32),
        ],
        compiler_params=pltpu.CompilerParams(
            dimension_semantics=("arbitrary", "arbitrary"),
            vmem_limit_bytes=VMEM_LIMIT),
        name="shortconv_mixer",
    )(*([h] * ns), g, win, cw, wout)


def _s5_prep_kernel(lr_ref, li_ref, ldt_ref, br_ref, bi_ref, ar_ref, ai_ref, bbr_ref, bbi_ref):
    lr = jnp.minimum(lr_ref[...], LAMBDA_RE_MAX)
    li = li_ref[...]
    dt = jnp.exp(ldt_ref[...])
    mag = jnp.exp(lr * dt)
    ab_re = mag * jnp.cos(li * dt)
    ab_im = mag * jnp.sin(li * dt)
    den = lr * lr + li * li
    nr = ab_re - 1.0
    ni = ab_im
    z_re = (nr * lr + ni * li) / den
    z_im = (ni * lr - nr * li) / den
    br = br_ref[...]
    bi = bi_ref[...]
    ar_ref[...] = ab_re
    ai_ref[...] = ab_im
    bbr_ref[...] = z_re * br - z_im * bi
    bbi_ref[...] = z_re * bi + z_im * br


def _s5_prep(lam_re, lam_im, log_dt, b_re, b_im):
    g, p = lam_re.shape
    n = g * p
    col = lambda a: a.reshape(n, 1)
    ldt = jnp.broadcast_to(log_dt[:, None], (g, p))
    outs = pl.pallas_call(
        _s5_prep_kernel,
        out_shape=[jax.ShapeDtypeStruct((n, 1), F32), jax.ShapeDtypeStruct((n, 1), F32),
                   jax.ShapeDtypeStruct((n, SSM_GROUP), F32),
                   jax.ShapeDtypeStruct((n, SSM_GROUP), F32)],
        name="s5_discretise",
    )(col(lam_re), col(lam_im), col(ldt), b_re.reshape(n, SSM_GROUP), b_im.reshape(n, SSM_GROUP))
    return outs


def _mix0_kernel(x_ref, g_ref, win_ref, bblk_ref, ar_ref, ai_ref, cblk_ref, dsk_ref,
                 wglu_ref, bglu_ref, ws_ref, bs_ref, gv_ref, wout_ref, o_ref,
                 xn_ref, up_ref, utb_ref, xs_ref, hst_ref, ytb_ref, vn_ref, gm_ref, s5_ref, acc_ref,
                 *, nb, tc, pitch, d, ssm_w, gm_w):
    rows = nb * tc
    nslab = xs_ref.shape[0]
    npair = nslab // 2
    uslab = ssm_w // LANES
    heads = gm_w // GMLP_HEAD

    @pl.when(pl.program_id(1) == 0)
    def _():
        hst_ref[...] = jnp.zeros_like(hst_ref)

    x = x_ref[...].reshape(rows, d)
    xn_ref[...] = _rms(x, g_ref[...]).astype(BF16)

    half = rows // 2
    for r0 in (0, half):
        u = jnp.dot(xn_ref[r0:r0 + half, :], win_ref[:, 0:ssm_w], preferred_element_type=F32)
        for b in range(r0 // tc, (r0 + half) // tc):
            for c in range(uslab):
                up_ref[c, b * pitch:b * pitch + tc, :] = u[b * tc - r0:(b + 1) * tc - r0,
                                                           c * LANES:(c + 1) * LANES]
    for t in range(tc):
        utb_ref[t * nb:(t + 1) * nb, :] = jnp.concatenate(
            [up_ref[c, pl.ds(t, nb, stride=pitch), :] for c in range(uslab)], axis=-1)

    ub = utb_ref[...].astype(BF16)
    for c in range(npair):
        xs = jnp.dot(ub, bblk_ref[:, c * 2 * LANES:(c + 1) * 2 * LANES], preferred_element_type=F32)
        xs_ref[2 * c] = xs[:, 0:LANES]
        xs_ref[2 * c + 1] = xs[:, LANES:2 * LANES]

    v0 = ssm_w + gm_w
    vv = _gelu(jnp.dot(xn_ref[...], win_ref[:, v0:v0 + gm_w], preferred_element_type=F32))
    vn_ref[...] = _rms(vv, gv_ref[...]).astype(BF16)

    ar = [jnp.broadcast_to(ar_ref[c], (nb, LANES)) for c in range(npair)]
    ai = [jnp.broadcast_to(ai_ref[c], (nb, LANES)) for c in range(npair)]
    h = [hst_ref[k] for k in range(nslab)]
    for t in range(tc):
        rt = slice(t * nb, (t + 1) * nb)
        for c in range(npair):
            hr, hi = h[2 * c], h[2 * c + 1]
            nr = ar[c] * hr - ai[c] * hi + xs_ref[2 * c, rt, :]
            ni = ar[c] * hi + ai[c] * hr + xs_ref[2 * c + 1, rt, :]
            xs_ref[2 * c, rt, :] = nr
            xs_ref[2 * c + 1, rt, :] = ni
            h[2 * c], h[2 * c + 1] = nr, ni
    for k in range(nslab):
        hst_ref[k] = h[k]

    rr = lax.broadcasted_iota(jnp.int32, (CHUNK, CHUNK), 0)
    cc = lax.broadcasted_iota(jnp.int32, (CHUNK, CHUNK), 1)
    tril = rr >= cc
    blocks = [b * tc + n * CHUNK for b in range(nb) for n in range(tc // CHUNK)]
    for hp in range(heads // 2):
        c0 = ssm_w + hp * 2 * GMLP_HEAD
        uu = _gelu(jnp.dot(xn_ref[...], win_ref[:, c0:c0 + 2 * GMLP_HEAD],
                           preferred_element_type=F32))
        for k in range(2):
            hh = 2 * hp + k
            w = jnp.where(tril, ws_ref[hh], 0.0).astype(BF16)
            vcols = slice(hh * GMLP_HEAD, (hh + 1) * GMLP_HEAD)
            rhs = jnp.concatenate([vn_ref[r0:r0 + CHUNK, vcols] for r0 in blocks], axis=1)
            gate = jnp.dot(w, rhs, preferred_element_type=F32) + bs_ref[hh]
            for i, r0 in enumerate(blocks):
                gm_ref[r0:r0 + CHUNK, hh * GMLP_HEAD:(hh + 1) * GMLP_HEAD] = (
                    uu[r0:r0 + CHUNK, k * GMLP_HEAD:(k + 1) * GMLP_HEAD]
                    * gate[:, i * GMLP_HEAD:(i + 1) * GMLP_HEAD]).astype(BF16)

    acc_ref[...] = x + jnp.dot(gm_ref[...], wout_ref[ssm_w:, :], preferred_element_type=F32)

    for r0 in (0, half):
        rs = slice(r0, r0 + half)
        hb = jnp.concatenate([xs_ref[k, rs, :] for k in range(nslab)], axis=-1).astype(BF16)
        y = jnp.dot(hb, cblk_ref[...], preferred_element_type=F32)
        y = _gelu(y + dsk_ref[...] * utb_ref[rs, :])
        gl = jnp.dot(y.astype(BF16), wglu_ref[...], preferred_element_type=F32) + bglu_ref[...]
        y = y * _sigmoid(gl)
        for c in range(uslab):
            ytb_ref[c, rs, :] = y[:, c * LANES:(c + 1) * LANES]
    for b in range(nb):
        for t0 in range(0, tc, 2 * SUBLANES):
            tile = jnp.concatenate(
                [jnp.concatenate([ytb_ref[c, pl.ds((t0 + dt) * nb + b, SUBLANES, stride=nb), :]
                                  for c in range(uslab)], axis=-1) for dt in (0, SUBLANES)], axis=0)
            s5_ref[b * tc + t0:b * tc + t0 + 2 * SUBLANES, :] = tile.astype(BF16)

    out = acc_ref[...] + jnp.dot(s5_ref[...], wout_ref[0:ssm_w, :], preferred_element_type=F32)
    o_ref[...] = out.reshape(nb, tc, d)


def _mix0_call(h, g, win, bblk, ar, ai, cblk, dsk, wglu, bglu, ws, bs, gv, wout, *, nb=8, tc=128):
    b, s, d = h.shape
    ssm_w = wglu.shape[0]
    gm_w = gv.shape[1]
    nslab = bblk.shape[1] // LANES
    pitch = tc + SUBLANES
    rows = nb * tc
    kern = functools.partial(_mix0_kernel, nb=nb, tc=tc, pitch=pitch, d=d, ssm_w=ssm_w, gm_w=gm_w)
    return pl.pallas_call(
        kern,
        grid=(b // nb, s // tc),
        in_specs=[
            pl.BlockSpec((nb, tc, d), lambda bi, ti: (bi, ti, 0)),
            _const_spec((1, d)),
            _const_spec(win.shape),
            _const_spec(bblk.shape),
            _const_spec(ar.shape),
            _const_spec(ai.shape),
            _const_spec(cblk.shape),
            _const_spec((1, ssm_w)),
            _const_spec(wglu.shape),
            _const_spec((1, ssm_w)),
            _const_spec(ws.shape),
            _const_spec(bs.shape),
            _const_spec((1, gm_w)),
            _const_spec(wout.shape),
        ],
        out_specs=pl.BlockSpec((nb, tc, d), lambda bi, ti: (bi, ti, 0)),
        out_shape=jax.ShapeDtypeStruct((b, s, d), F32),
        scratch_shapes=[
            pltpu.VMEM((rows, d), BF16),
            pltpu.VMEM((ssm_w // LANES, nb * pitch, LANES), F32),
            pltpu.VMEM((rows, ssm_w), F32),
            pltpu.VMEM((nslab, rows, LANES), F32),
            pltpu.VMEM((nslab, nb, LANES), F32),
            pltpu.VMEM((ssm_w // LANES, rows, LANES), F32),
            pltpu.VMEM((rows, gm_w), BF16),
            pltpu.VMEM((rows, gm_w), BF16),
            pltpu.VMEM((rows, ssm_w), BF16),
            pltpu.VMEM((rows, d), F32),
        ],
        compiler_params=pltpu.CompilerParams(
            dimension_semantics=("arbitrary", "arbitrary"),
            vmem_limit_bytes=VMEM_LIMIT),
        name="s5_gmlp_mixer",
    )(h, g, win, bblk, ar, ai, cblk, dsk, wglu, bglu, ws, bs, gv, wout)


def _s5_operands(lam_re, lam_im, log_dt, b_re, b_im, c_re, c_im):
    g, p = lam_re.shape
    hdim = b_re.shape[-1]
    n = g * p
    npair = n // LANES
    ab_re, ab_im, bb_re, bb_im = _s5_prep(lam_re, lam_im, log_dt, b_re, b_im)
    eye = jnp.eye(g, dtype=F32)

    def b_dense(bb):
        bb = bb.reshape(g, p, hdim)
        return jnp.einsum('gph,gk->ghkp', bb, eye).reshape(g * hdim, npair, LANES)

    def c_dense(cm):
        return jnp.einsum('ghp,gk->gpkh', cm, eye).reshape(npair, LANES, g * hdim)

    bblk = jnp.stack([b_dense(bb_re), b_dense(bb_im)], axis=2).reshape(g * hdim, 2 * n)
    cblk = jnp.stack([c_dense(c_re), -c_dense(c_im)], axis=1).reshape(2 * n, g * hdim)
    ar = ab_re.reshape(npair, 1, LANES)
    ai = ab_im.reshape(npair, 1, LANES)
    return bblk.astype(BF16), ar, ai, cblk.astype(BF16)


def kernel(x, mix_norm_g, ffn_norm_g, final_norm_g, ev_w_in, ev_w_out, s5_lam_re, s5_lam_im,
           s5_log_dt, s5_b_re, s5_b_im, s5_c_re, s5_c_im, s5_d, s5_w_glu, s5_b_glu, gm_w_s,
           gm_b_s, gm_v_g, od_w_in, od_conv_w, od_conv_b, od_w_out, ffn_w_up, ffn_conv_w,
           ffn_conv_b, ffn_w_down):
    depth = mix_norm_g.shape[0]
    row = lambda a: a.reshape(1, -1).astype(F32)

    def col_tiles(w, parts, tf):
        k, n = w.shape[0], w.shape[1] // parts
        return w.reshape(k, parts, n // tf, tf).transpose(2, 0, 1, 3).reshape(n // tf, k, parts * tf)

    def conv_tiles(cw, cb, parts, tf):
        t = col_tiles(jnp.concatenate([cw, cb[None, :]], axis=0).astype(F32), parts, tf)
        return jnp.broadcast_to(t[:, :, None, :], t.shape[:2] + (SUBLANES, t.shape[2]))

    h = x
    for layer in range(depth):
        if layer % 2 == 0:
            e = layer // 2
            bblk, ar, ai, cblk = _s5_operands(s5_lam_re[e], s5_lam_im[e], s5_log_dt[e],
                                              s5_b_re[e], s5_b_im[e], s5_c_re[e], s5_c_im[e])
            h = _mix0_call(h, row(mix_norm_g[layer]), ev_w_in[e].astype(BF16), bblk, ar, ai, cblk,
                           row(s5_d[e]), s5_w_glu[e].astype(BF16), row(s5_b_glu[e]),
                           gm_w_s[e], gm_b_s[e][:, :, None], row(gm_v_g[e]),
                           ev_w_out[e].astype(BF16))
        else:
            o = layer // 2
            h = _shortconv_call(h, row(mix_norm_g[layer]),
                                od_w_in[o].astype(BF16),
                                conv_tiles(od_conv_w[o], od_conv_b[o], 1, MIX_TILE),
                                od_w_out[o].astype(BF16))
        gf = row(final_norm_g) if layer == depth - 1 else None
        h = _ffn_call(h, row(ffn_norm_g[layer]),
                      ffn_w_up[layer].astype(BF16),
                      conv_tiles(ffn_conv_w[layer], ffn_conv_b[layer], 2, FFN_TILE),
                      ffn_w_down[layer].astype(BF16), gf)
    return h
```

```python
import functools
import math

import jax
import jax.numpy as jnp
from jax import lax
from jax.experimental import pallas as pl
from jax.experimental.pallas import tpu as pltpu

F32 = jnp.float32
BF16 = jnp.bfloat16

EPS = 1e-6
LAMBDA_RE_MAX = -1e-4
LANES = 128
SUBLANES = 8
HALO = 2 * SUBLANES
SUB_ROWS = SUBLANES * SUBLANES
WORK_SUB = HALO + SUB_ROWS
ROW_CHUNK = 32
FFN_TILE = 256
FFN_GROUP = 4
MIX_TILE = 256
MIX_GROUP = 2
SSM_GROUP = 16
SSM_STATE = 64
GMLP_HEAD = 128
CHUNK = 128
VMEM_LIMIT = 56 * 1024 * 1024


def _rms(x, g):
    ms = jnp.mean(x * x, axis=-1, keepdims=True)
    return x * lax.rsqrt(ms + EPS) * g


def _gelu(x):
    c = math.sqrt(2.0 / math.pi)
    return 0.5 * x * (1.0 + jnp.tanh(c * (x + 0.044715 * (x * x * x))))


def _sigmoid(x):
    return 1.0 / (1.0 + jnp.exp(-x))


def _const_spec(shape):
    nd = len(shape)
    return pl.BlockSpec(shape, lambda *_: (0,) * nd, pipeline_mode=pl.Buffered(1))


def _transposed_tile(slab_ref, sb, k):
    return slab_ref[pl.ds(sb * SUB_ROWS + k, SUBLANES, stride=SUBLANES), :]


def _load_rows_permuted(x_refs, g, xn_ref):
    for t in range(0, xn_ref.shape[0] // SUBLANES, 2):
        xp = jnp.concatenate(
            [jnp.concatenate([_transposed_tile(xr, tt // SUBLANES, tt % SUBLANES) for xr in x_refs],
                             axis=-1) for tt in (t, t + 1)], axis=0)
        xn_ref[t * SUBLANES:(t + 2) * SUBLANES, :] = _rms(xp, g).astype(BF16)


def _store_with_halo(work, carry_ref, col0, hp):
    tm, w = hp.shape
    sub = lax.broadcasted_iota(jnp.int32, (SUBLANES, w), 0)
    for sb in range(tm // SUB_ROWS):
        base = sb * WORK_SUB
        work[base + HALO:base + WORK_SUB, :] = hp[sb * SUB_ROWS:(sb + 1) * SUB_ROWS, :]
        for k in range(2):
            tail = work[base + WORK_SUB - HALO + k * SUBLANES:base + WORK_SUB - HALO + (k + 1) * SUBLANES, :]
            if sb == 0:
                prev = carry_ref[k * SUBLANES:(k + 1) * SUBLANES, col0:col0 + w]
            else:
                prev = work[base - HALO + k * SUBLANES:base - HALO + (k + 1) * SUBLANES, :]
            work[base + k * SUBLANES:base + (k + 1) * SUBLANES, :] = pltpu.roll(
                jnp.where(sub == SUBLANES - 1, prev, tail), 1, 0)
    last = (tm // SUB_ROWS) * WORK_SUB
    carry_ref[:, col0:col0 + w] = work[last - HALO:last, :]


def _conv_rows(work, cw, sb, r, n):
    s = SUBLANES
    r0 = sb * WORK_SUB + r
    tiles = lambda a: a.reshape(n // s, s, a.shape[-1])
    c = (cw[3] + cw[0] * tiles(work[r0:r0 + n, :]) + cw[1] * tiles(work[s + r0:s + r0 + n, :])
         + cw[2] * tiles(work[2 * s + r0:2 * s + r0 + n, :]))
    return c.reshape(n, c.shape[-1])


def _acc_slabs(acc_ref, dd, first):
    for c in range(acc_ref.shape[0]):
        if first:
            acc_ref[c] = dd[:, c * LANES:(c + 1) * LANES]
        else:
            acc_ref[c] += dd[:, c * LANES:(c + 1) * LANES]


def _natural_tile(acc_ref, t):
    return jnp.concatenate([_transposed_tile(acc_ref.at[c], t // SUBLANES, t % SUBLANES)
                            for c in range(acc_ref.shape[0])], axis=-1)


def _run_tiles(nt, dk, up, gate, down):
    up(0)
    for j in range(nt):
        if j + 1 < nt:
            up(j + 1)
        if j > 0 and j % dk == 0:
            down(j // dk - 1)
        gate(j)
    down((nt - 1) // dk)


def _down_group(a_ref, w_ref, acc_ref, g, nt, dk, tf):
    n = min(dk, nt - g * dk)
    dd = jnp.dot(a_ref[g % 2, :, 0:n * tf], w_ref[g * dk * tf:(g * dk + n) * tf, :],
                 preferred_element_type=F32)
    _acc_slabs(acc_ref, dd, first=(g == 0))


def _ffn_kernel(*refs, tm, tf, nt, dk, d, final):
    ns = d // LANES
    x_refs, refs = refs[:ns], refs[ns:]
    if final:
        (g_ref, wup_ref, cw_ref, wdn_ref, gf_ref, o_ref,
         xn_ref, carry_ref, work_ref, a_ref, acc_ref) = refs
    else:
        (g_ref, wup_ref, cw_ref, wdn_ref, o_ref,
         xn_ref, carry_ref, work_ref, a_ref, acc_ref) = refs

    @pl.when(pl.program_id(1) == 0)
    def _():
        carry_ref[...] = jnp.zeros_like(carry_ref)

    _load_rows_permuted(x_refs, g_ref[...], xn_ref)

    def up(j):
        dff = nt * tf
        for part in range(2):
            c0 = part * dff + j * tf
            hp = jnp.dot(xn_ref[...], wup_ref[:, c0:c0 + tf], preferred_element_type=F32)
            _store_with_halo(work_ref.at[j % 2, :, part * tf:(part + 1) * tf], carry_ref, c0, hp)

    def gate(j):
        work, cw = work_ref.at[j % 2], cw_ref.at[j]
        for r in range(0, tm, ROW_CHUNK):
            c = _conv_rows(work, cw, r // SUB_ROWS, r % SUB_ROWS, ROW_CHUNK)
            gt, vl = c[:, :tf], c[:, tf:]
            a_ref[(j // dk) % 2, r:r + ROW_CHUNK, (j % dk) * tf:(j % dk + 1) * tf] = (
                gt * _sigmoid(gt) * vl).astype(BF16)

    down = functools.partial(_down_group, a_ref, wdn_ref, acc_ref, nt=nt, dk=dk, tf=tf)
    _run_tiles(nt, dk, up, gate, down)

    for t in range(tm // SUBLANES):
        rows = slice(t * SUBLANES, (t + 1) * SUBLANES)
        out = jnp.concatenate([xr[rows, :] for xr in x_refs], axis=-1) + _natural_tile(acc_ref, t)
        if final:
            out = _rms(out, gf_ref[...])
        o_ref[rows, :] = out


def _slab_specs(tm, d):
    return [pl.BlockSpec((None, tm, LANES), lambda bi, ti, c=c: (bi, ti, c)) for c in range(d // LANES)]


def _ffn_call(h, g, wup, cw, wdn, gf=None, *, tm=512, dk=FFN_GROUP):
    b, s, d = h.shape
    tm = min(tm, s)
    nt, tf = cw.shape[0], cw.shape[3] // 2
    ns = d // LANES
    final = gf is not None
    in_specs = _slab_specs(tm, d) + [_const_spec((1, d)), _const_spec(wup.shape),
                                     _const_spec(cw.shape), _const_spec(wdn.shape)]
    args = [h] * ns + [g, wup, cw, wdn]
    if final:
        in_specs.append(_const_spec((1, d)))
        args.append(gf)
    return pl.pallas_call(
        functools.partial(_ffn_kernel, tm=tm, tf=tf, nt=nt, dk=dk, d=d, final=final),
        grid=(b, s // tm),
        in_specs=in_specs,
        out_specs=pl.BlockSpec((None, tm, d), lambda bi, ti: (bi, ti, 0)),
        out_shape=jax.ShapeDtypeStruct((b, s, d), F32),
        scratch_shapes=[
            pltpu.VMEM((tm, d), BF16),
            pltpu.VMEM((HALO, nt * 2 * tf), F32),
            pltpu.VMEM((2, tm // SUB_ROWS * WORK_SUB, 2 * tf), F32),
            pltpu.VMEM((2, tm, dk * tf), BF16),
            pltpu.VMEM((ns, tm, LANES), F32),
        ],
        compiler_params=pltpu.CompilerParams(
            dimension_semantics=("arbitrary", "arbitrary"),
            vmem_limit_bytes=VMEM_LIMIT),
        name="conv_ffn_final" if final else "conv_ffn",
    )(*args)


def _shortconv_kernel(*refs, tm, tf, nt, dk, d):
    ns = d // LANES
    x_refs, refs = refs[:ns], refs[ns:]
    (g_ref, win_ref, cw_ref, wout_ref, o_ref,
     xn_ref, carry_ref, work_ref, bg_ref, a_ref, acc_ref) = refs

    @pl.when(pl.program_id(1) == 0)
    def _():
        carry_ref[...] = jnp.zeros_like(carry_ref)

    _load_rows_permuted(x_refs, g_ref[...], xn_ref)

    def up(j):
        bg, cg, hx = [jnp.dot(xn_ref[...], win_ref[:, part * d + j * tf:part * d + (j + 1) * tf],
                              preferred_element_type=F32) for part in range(3)]
        bg_ref[j % 2] = bg
        _store_with_halo(work_ref.at[j % 2], carry_ref, j * tf, cg * hx)

    def gate(j):
        work, cw = work_ref.at[j % 2], cw_ref.at[j]
        for r in range(0, tm, ROW_CHUNK):
            c = _conv_rows(work, cw, r // SUB_ROWS, r % SUB_ROWS, ROW_CHUNK)
            a_ref[(j // dk) % 2, r:r + ROW_CHUNK, (j % dk) * tf:(j % dk + 1) * tf] = (
                bg_ref[j % 2, r:r + ROW_CHUNK, :] * c).astype(BF16)

    down = functools.partial(_down_group, a_ref, wout_ref, acc_ref, nt=nt, dk=dk, tf=tf)
    _run_tiles(nt, dk, up, gate, down)

    for t in range(tm // SUBLANES):
        rows = slice(t * SUBLANES, (t + 1) * SUBLANES)
        o_ref[rows, :] = (jnp.concatenate([xr[rows, :] for xr in x_refs], axis=-1)
                          + _natural_tile(acc_ref, t))


def _shortconv_call(h, g, win, cw, wout, *, tm=512, dk=MIX_GROUP):
    b, s, d = h.shape
    tm = min(tm, s)
    nt, tf = cw.shape[0], cw.shape[3]
    ns = d // LANES
    return pl.pallas_call(
        functools.partial(_shortconv_kernel, tm=tm, tf=tf, nt=nt, dk=dk, d=d),
        grid=(b, s // tm),
        in_specs=_slab_specs(tm, d) + [_const_spec((1, d)), _const_spec(win.shape),
                                       _const_spec(cw.shape), _const_spec(wout.shape)],
        out_specs=pl.BlockSpec((None, tm, d), lambda bi, ti: (bi, ti, 0)),
        out_shape=jax.ShapeDtypeStruct((b, s, d), F32),
        scratch_shapes=[
            pltpu.VMEM((tm, d), BF16),
            pltpu.VMEM((HALO, nt * tf), F32),
            pltpu.VMEM((2, tm // SUB_ROWS * WORK_SUB, tf), F32),
            pltpu.VMEM((2, tm, tf), F32),
            pltpu.VMEM((2, tm, dk * tf), BF16),
            pltpu.VMEM((ns, tm, LANES), F32),
        ],
        compiler_params=pltpu.CompilerParams(
            dimension_semantics=("arbitrary", "arbitrary"),
            vmem_limit_bytes=VMEM_LIMIT),
        name="shortconv_mixer",
    )(*([h] * ns), g, win, cw, wout)


def _s5_prep_kernel(lr_ref, li_ref, ldt_ref, br_ref, bi_ref, ar_ref, ai_ref, bbr_ref, bbi_ref):
    lr = jnp.minimum(lr_ref[...], LAMBDA_RE_MAX)
    li = li_ref[...]
    dt = jnp.exp(ldt_ref[...])
    mag = jnp.exp(lr * dt)
    ab_re = mag * jnp.cos(li * dt)
    ab_im = mag * jnp.sin(li * dt)
    den = lr * lr + li * li
    nr = ab_re - 1.0
    ni = ab_im
    z_re = (nr * lr + ni * li) / den
    z_im = (ni * lr - nr * li) / den
    br = br_ref[...]
    bi = bi_ref[...]
    ar_ref[...] = ab_re
    ai_ref[...] = ab_im
    bbr_ref[...] = z_re * br - z_im * bi
    bbi_ref[...] = z_re * bi + z_im * br


def _s5_prep(lam_re, lam_im, log_dt, b_re, b_im):
    g, p = lam_re.shape
    n = g * p
    col = lambda a: a.reshape(n, 1)
    ldt = jnp.broadcast_to(log_dt[:, None], (g, p))
    outs = pl.pallas_call(
        _s5_prep_kernel,
        out_shape=[jax.ShapeDtypeStruct((n, 1), F32), jax.ShapeDtypeStruct((n, 1), F32),
                   jax.ShapeDtypeStruct((n, SSM_GROUP), F32),
                   jax.ShapeDtypeStruct((n, SSM_GROUP), F32)],
        name="s5_discretise",
    )(col(lam_re), col(lam_im), col(ldt), b_re.reshape(n, SSM_GROUP), b_im.reshape(n, SSM_GROUP))
    return outs


def _mix0_kernel(x_ref, g_ref, win_ref, bblk_ref, ar_ref, ai_ref, cblk_ref, dsk_ref,
                 wglu_ref, bglu_ref, ws_ref, bs_ref, gv_ref, wout_ref, o_ref,
                 xn_ref, up_ref, utb_ref, xs_ref, hst_ref, ytb_ref, vn_ref, gm_ref, s5_ref, acc_ref,
                 *, nb, tc, pitch, d, ssm_w, gm_w):
    rows = nb * tc
    nslab = xs_ref.shape[0]
    npair = nslab // 2
    uslab = ssm_w // LANES
    heads = gm_w // GMLP_HEAD

    @pl.when(pl.program_id(1) == 0)
    def _():
        hst_ref[...] = jnp.zeros_like(hst_ref)

    x = x_ref[...].reshape(rows, d)
    xn_ref[...] = _rms(x, g_ref[...]).astype(BF16)

    half = rows // 2
    us = [jnp.dot(xn_ref[r0:r0 + half, :], win_ref[:, 0:ssm_w], preferred_element_type=F32)
          for r0 in (0, half)]
    for b in range(nb):
        u, r0 = us[b * tc // half], b * tc % half
        for c in range(uslab):
            up_ref[c, b * pitch:b * pitch + tc, :] = u[r0:r0 + tc, c * LANES:(c + 1) * LANES]
    for t in range(tc):
        utb_ref[t * nb:(t + 1) * nb, :] = jnp.concatenate(
            [up_ref[c, pl.ds(t, nb, stride=pitch), :] for c in range(uslab)], axis=-1)

    ub = utb_ref[...].astype(BF16)
    for c in range(npair):
        xs = jnp.dot(ub, bblk_ref[:, c * 2 * LANES:(c + 1) * 2 * LANES], preferred_element_type=F32)
        xs_ref[2 * c] = xs[:, 0:LANES]
        xs_ref[2 * c + 1] = xs[:, LANES:2 * LANES]

    v0 = ssm_w + gm_w
    vv = _gelu(jnp.dot(xn_ref[...], win_ref[:, v0:v0 + gm_w], preferred_element_type=F32))
    vn_ref[...] = _rms(vv, gv_ref[...]).astype(BF16)

    ar = [jnp.broadcast_to(ar_ref[c], (nb, LANES)) for c in range(npair)]
    ai = [jnp.broadcast_to(ai_ref[c], (nb, LANES)) for c in range(npair)]
    h = [hst_ref[k] for k in range(nslab)]
    for t in range(tc):
        rt = slice(t * nb, (t + 1) * nb)
        for c in range(npair):
            hr, hi = h[2 * c], h[2 * c + 1]
            nr = ar[c] * hr - ai[c] * hi + xs_ref[2 * c, rt, :]
            ni = ar[c] * hi + ai[c] * hr + xs_ref[2 * c + 1, rt, :]
            xs_ref[2 * c, rt, :] = nr
            xs_ref[2 * c + 1, rt, :] = ni
            h[2 * c], h[2 * c + 1] = nr, ni
    for k in range(nslab):
        hst_ref[k] = h[k]

    rr = lax.broadcasted_iota(jnp.int32, (CHUNK, CHUNK), 0)
    cc = lax.broadcasted_iota(jnp.int32, (CHUNK, CHUNK), 1)
    tril = rr >= cc
    blocks = [b * tc + n * CHUNK for b in range(nb) for n in range(tc // CHUNK)]
    uus = [_gelu(jnp.dot(xn_ref[...], win_ref[:, ssm_w + hp * 2 * GMLP_HEAD:ssm_w + (hp + 1) * 2 * GMLP_HEAD],
                         preferred_element_type=F32)) for hp in range(heads // 2)]
    for hh in range(heads):
        uu, k = uus[hh // 2], hh % 2
        w = jnp.where(tril, ws_ref[hh], 0.0).astype(BF16)
        vcols = slice(hh * GMLP_HEAD, (hh + 1) * GMLP_HEAD)
        rhs = jnp.concatenate([vn_ref[r0:r0 + CHUNK, vcols] for r0 in blocks], axis=1)
        gate = jnp.dot(w, rhs, preferred_element_type=F32) + bs_ref[hh]
        for i, r0 in enumerate(blocks):
            gm_ref[r0:r0 + CHUNK, vcols] = (
                uu[r0:r0 + CHUNK, k * GMLP_HEAD:(k + 1) * GMLP_HEAD]
                * gate[:, i * GMLP_HEAD:(i + 1) * GMLP_HEAD]).astype(BF16)

    acc_ref[...] = x + jnp.dot(gm_ref[...], wout_ref[ssm_w:, :], preferred_element_type=F32)

    halves = [slice(r0, r0 + half) for r0 in (0, half)]
    ys = [jnp.dot(jnp.concatenate([xs_ref[k, rs, :] for k in range(nslab)], axis=-1).astype(BF16),
                  cblk_ref[...], preferred_element_type=F32) for rs in halves]
    ys = [_gelu(y + dsk_ref[...] * utb_ref[rs, :]) for y, rs in zip(ys, halves)]
    gls = [jnp.dot(y.astype(BF16), wglu_ref[...], preferred_element_type=F32) + bglu_ref[...]
           for y in ys]
    for y, gl, rs in zip(ys, gls, halves):
        y = y * _sigmoid(gl)
        for c in range(uslab):
            ytb_ref[c, rs, :] = y[:, c * LANES:(c + 1) * LANES]
    for b in range(nb):
        for t0 in range(0, tc, 2 * SUBLANES):
            tile = jnp.concatenate(
                [jnp.concatenate([ytb_ref[c, pl.ds((t0 + dt) * nb + b, SUBLANES, stride=nb), :]
                                  for c in range(uslab)], axis=-1) for dt in (0, SUBLANES)], axis=0)
            s5_ref[b * tc + t0:b * tc + t0 + 2 * SUBLANES, :] = tile.astype(BF16)

    out = acc_ref[...] + jnp.dot(s5_ref[...], wout_ref[0:ssm_w, :], preferred_element_type=F32)
    o_ref[...] = out.reshape(nb, tc, d)


def _mix0_call(h, g, win, bblk, ar, ai, cblk, dsk, wglu, bglu, ws, bs, gv, wout, *, nb=8, tc=128):
    b, s, d = h.shape
    ssm_w = wglu.shape[0]
    gm_w = gv.shape[1]
    nslab = bblk.shape[1] // LANES
    pitch = tc + SUBLANES
    rows = nb * tc
    kern = functools.partial(_mix0_kernel, nb=nb, tc=tc, pitch=pitch, d=d, ssm_w=ssm_w, gm_w=gm_w)
    return pl.pallas_call(
        kern,
        grid=(b // nb, s // tc),
        in_specs=[
            pl.BlockSpec((nb, tc, d), lambda bi, ti: (bi, ti, 0)),
            _const_spec((1, d)),
            _const_spec(win.shape),
            _const_spec(bblk.shape),
            _const_spec(ar.shape),
            _const_spec(ai.shape),
            _const_spec(cblk.shape),
            _const_spec((1, ssm_w)),
            _const_spec(wglu.shape),
            _const_spec((1, ssm_w)),
            _const_spec(ws.shape),
            _const_spec(bs.shape),
            _const_spec((1, gm_w)),
            _const_spec(wout.shape),
        ],
        out_specs=pl.BlockSpec((nb, tc, d), lambda bi, ti: (bi, ti, 0)),
        out_shape=jax.ShapeDtypeStruct((b, s, d), F32),
        scratch_shapes=[
            pltpu.VMEM((rows, d), BF16),
            pltpu.VMEM((ssm_w // LANES, nb * pitch, LANES), F32),
            pltpu.VMEM((rows, ssm_w), F32),
            pltpu.VMEM((nslab, rows, LANES), F32),
            pltpu.VMEM((nslab, nb, LANES), F32),
            pltpu.VMEM((ssm_w // LANES, rows, LANES), F32),
            pltpu.VMEM((rows, gm_w), BF16),
            pltpu.VMEM((rows, gm_w), BF16),
            pltpu.VMEM((rows, ssm_w), BF16),
            pltpu.VMEM((rows, d), F32),
        ],
        compiler_params=pltpu.CompilerParams(
            dimension_semantics=("arbitrary", "arbitrary"),
            vmem_limit_bytes=VMEM_LIMIT),
        name="s5_gmlp_mixer",
    )(h, g, win, bblk, ar, ai, cblk, dsk, wglu, bglu, ws, bs, gv, wout)


def _s5_operands(lam_re, lam_im, log_dt, b_re, b_im, c_re, c_im):
    g, p = lam_re.shape
    hdim = b_re.shape[-1]
    n = g * p
    npair = n // LANES
    ab_re, ab_im, bb_re, bb_im = _s5_prep(lam_re, lam_im, log_dt, b_re, b_im)
    eye = jnp.eye(g, dtype=F32)

    def b_dense(bb):
        bb = bb.reshape(g, p, hdim)
        return jnp.einsum('gph,gk->ghkp', bb, eye).reshape(g * hdim, npair, LANES)

    def c_dense(cm):
        return jnp.einsum('ghp,gk->gpkh', cm, eye).reshape(npair, LANES, g * hdim)

    bblk = jnp.stack([b_dense(bb_re), b_dense(bb_im)], axis=2).reshape(g * hdim, 2 * n)
    cblk = jnp.stack([c_dense(c_re), -c_dense(c_im)], axis=1).reshape(2 * n, g * hdim)
    ar = ab_re.reshape(npair, 1, LANES)
    ai = ab_im.reshape(npair, 1, LANES)
    return bblk.astype(BF16), ar, ai, cblk.astype(BF16)


def kernel(x, mix_norm_g, ffn_norm_g, final_norm_g, ev_w_in, ev_w_out, s5_lam_re, s5_lam_im,
           s5_log_dt, s5_b_re, s5_b_im, s5_c_re, s5_c_im, s5_d, s5_w_glu, s5_b_glu, gm_w_s,
           gm_b_s, gm_v_g, od_w_in, od_conv_w, od_conv_b, od_w_out, ffn_w_up, ffn_conv_w,
           ffn_conv_b, ffn_w_down):
    depth = mix_norm_g.shape[0]
    row = lambda a: a.reshape(1, -1).astype(F32)

    def col_tiles(w, parts, tf):
        k, n = w.shape[0], w.shape[1] // parts
        return w.reshape(k, parts, n // tf, tf).transpose(2, 0, 1, 3).reshape(n // tf, k, parts * tf)

    def conv_tiles(cw, cb, parts, tf):
        t = col_tiles(jnp.concatenate([cw, cb[None, :]], axis=0).astype(F32), parts, tf)
        return jnp.broadcast_to(t[:, :, None, :], t.shape[:2] + (SUBLANES, t.shape[2]))

    h = x
    for layer in range(depth):
        if layer % 2 == 0:
            e = layer // 2
            bblk, ar, ai, cblk = _s5_operands(s5_lam_re[e], s5_lam_im[e], s5_log_dt[e],
                                              s5_b_re[e], s5_b_im[e], s5_c_re[e], s5_c_im[e])
            h = _mix0_call(h, row(mix_norm_g[layer]), ev_w_in[e].astype(BF16), bblk, ar, ai, cblk,
                           row(s5_d[e]), s5_w_glu[e].astype(BF16), row(s5_b_glu[e]),
                           gm_w_s[e], gm_b_s[e][:, :, None], row(gm_v_g[e]),
                           ev_w_out[e].astype(BF16))
        else:
            o = layer // 2
            h = _shortconv_call(h, row(mix_norm_g[layer]),
                                od_w_in[o].astype(BF16),
                                conv_tiles(od_conv_w[o], od_conv_b[o], 1, MIX_TILE),
                                od_w_out[o].astype(BF16))
        gf = row(final_norm_g) if layer == depth - 1 else None
        h = _ffn_call(h, row(ffn_norm_g[layer]),
                      ffn_w_up[layer].astype(BF16),
                      conv_tiles(ffn_conv_w[layer], ffn_conv_b[layer], 2, FFN_TILE),
                      ffn_w_down[layer].astype(BF16), gf)
    return h
```

```python
import functools
import math

import jax
import jax.numpy as jnp
from jax import lax
from jax.experimental import pallas as pl
from jax.experimental.pallas import tpu as pltpu

F32 = jnp.float32
BF16 = jnp.bfloat16

EPS = 1e-6
LAMBDA_RE_MAX = -1e-4
LANES = 128
SUBLANES = 8
HALO = 2 * SUBLANES
SUB_ROWS = SUBLANES * SUBLANES
WORK_SUB = HALO + SUB_ROWS
ROW_CHUNK = 32
CONV_ROWS = 1024
FFN_TILE = 256
FFN_GROUP = 4
MIX_TILE = 256
MIX_GROUP = 2
SSM_GROUP = 16
SSM_STATE = 64
GMLP_HEAD = 128
CHUNK = 128
VMEM_LIMIT = 56 * 1024 * 1024


def _rms(x, g):
    ms = jnp.mean(x * x, axis=-1, keepdims=True)
    return x * lax.rsqrt(ms + EPS) * g


def _gelu(x):
    c = math.sqrt(2.0 / math.pi)
    return 0.5 * x * (1.0 + jnp.tanh(c * (x + 0.044715 * (x * x * x))))


def _sigmoid(x):
    return 1.0 / (1.0 + jnp.exp(-x))


def _const_spec(shape):
    nd = len(shape)
    return pl.BlockSpec(shape, lambda *_: (0,) * nd, pipeline_mode=pl.Buffered(1))


def _transposed_tile(slab_ref, sb, k):
    return slab_ref[pl.ds(sb * SUB_ROWS + k, SUBLANES, stride=SUBLANES), :]


def _load_rows_permuted(x_refs, g, xn_ref):
    for t in range(0, xn_ref.shape[0] // SUBLANES, 2):
        xp = jnp.concatenate(
            [jnp.concatenate([_transposed_tile(xr, tt // SUBLANES, tt % SUBLANES) for xr in x_refs],
                             axis=-1) for tt in (t, t + 1)], axis=0)
        xn_ref[t * SUBLANES:(t + 2) * SUBLANES, :] = _rms(xp, g).astype(BF16)


def _store_with_halo(work, carry_ref, col0, hp):
    tm, w = hp.shape
    sub = lax.broadcasted_iota(jnp.int32, (SUBLANES, w), 0)
    for sb in range(tm // SUB_ROWS):
        base = sb * WORK_SUB
        work[base + HALO:base + WORK_SUB, :] = hp[sb * SUB_ROWS:(sb + 1) * SUB_ROWS, :]
        for k in range(2):
            tail = work[base + WORK_SUB - HALO + k * SUBLANES:base + WORK_SUB - HALO + (k + 1) * SUBLANES, :]
            if sb == 0:
                prev = carry_ref[k * SUBLANES:(k + 1) * SUBLANES, col0:col0 + w]
            else:
                prev = work[base - HALO + k * SUBLANES:base - HALO + (k + 1) * SUBLANES, :]
            work[base + k * SUBLANES:base + (k + 1) * SUBLANES, :] = pltpu.roll(
                jnp.where(sub == SUBLANES - 1, prev, tail), 1, 0)
    last = (tm // SUB_ROWS) * WORK_SUB
    carry_ref[:, col0:col0 + w] = work[last - HALO:last, :]


def _conv_rows(work, cw, sb, r, n):
    s = SUBLANES
    r0 = sb * WORK_SUB + r
    tiles = lambda a: a.reshape(n // s, s, a.shape[-1])
    c = (cw[3] + cw[0] * tiles(work[r0:r0 + n, :]) + cw[1] * tiles(work[s + r0:s + r0 + n, :])
         + cw[2] * tiles(work[2 * s + r0:2 * s + r0 + n, :]))
    return c.reshape(n, c.shape[-1])


def _acc_slabs(acc_ref, dd, first):
    for c in range(acc_ref.shape[0]):
        if first:
            acc_ref[c] = dd[:, c * LANES:(c + 1) * LANES]
        else:
            acc_ref[c] += dd[:, c * LANES:(c + 1) * LANES]


def _natural_tile(acc_ref, t):
    return jnp.concatenate([_transposed_tile(acc_ref.at[c], t // SUBLANES, t % SUBLANES)
                            for c in range(acc_ref.shape[0])], axis=-1)


def _run_tiles(nt, dk, up, gate, down):
    up(0)
    for j in range(nt):
        if j + 1 < nt:
            up(j + 1)
        if j > 0 and j % dk == 0:
            down(j // dk - 1)
        gate(j)
    down((nt - 1) // dk)


def _down_group(a_ref, w_ref, acc_ref, g, nt, dk, tf):
    n = min(dk, nt - g * dk)
    dd = jnp.dot(a_ref[g % 2, :, 0:n * tf], w_ref[g * dk * tf:(g * dk + n) * tf, :],
                 preferred_element_type=F32)
    _acc_slabs(acc_ref, dd, first=(g == 0))


def _ffn_kernel(*refs, tm, tf, nt, dk, d, final):
    ns = d // LANES
    x_refs, refs = refs[:ns], refs[ns:]
    if final:
        (g_ref, wup_ref, cw_ref, wdn_ref, gf_ref, o_ref,
         xn_ref, carry_ref, work_ref, a_ref, acc_ref) = refs
    else:
        (g_ref, wup_ref, cw_ref, wdn_ref, o_ref,
         xn_ref, carry_ref, work_ref, a_ref, acc_ref) = refs

    @pl.when(pl.program_id(1) == 0)
    def _():
        carry_ref[...] = jnp.zeros_like(carry_ref)

    _load_rows_permuted(x_refs, g_ref[...], xn_ref)

    def up(j):
        dff = nt * tf
        for part in range(2):
            c0 = part * dff + j * tf
            hp = jnp.dot(xn_ref[...], wup_ref[:, c0:c0 + tf], preferred_element_type=F32)
            _store_with_halo(work_ref.at[j % 2, :, part * tf:(part + 1) * tf], carry_ref, c0, hp)

    def gate(j):
        work, cw = work_ref.at[j % 2], cw_ref.at[j]
        for r in range(0, tm, ROW_CHUNK):
            c = _conv_rows(work, cw, r // SUB_ROWS, r % SUB_ROWS, ROW_CHUNK)
            gt, vl = c[:, :tf], c[:, tf:]
            a_ref[(j // dk) % 2, r:r + ROW_CHUNK, (j % dk) * tf:(j % dk + 1) * tf] = (
                gt * _sigmoid(gt) * vl).astype(BF16)

    down = functools.partial(_down_group, a_ref, wdn_ref, acc_ref, nt=nt, dk=dk, tf=tf)
    _run_tiles(nt, dk, up, gate, down)

    for t in range(tm // SUBLANES):
        rows = slice(t * SUBLANES, (t + 1) * SUBLANES)
        out = jnp.concatenate([xr[rows, :] for xr in x_refs], axis=-1) + _natural_tile(acc_ref, t)
        if final:
            out = _rms(out, gf_ref[...])
        o_ref[rows, :] = out


def _slab_specs(tm, d):
    return [pl.BlockSpec((None, tm, LANES), lambda bi, ti, c=c: (bi, ti, c)) for c in range(d // LANES)]


def _ffn_call(h, g, wup, cw, wdn, gf=None, *, tm=CONV_ROWS, dk=FFN_GROUP):
    b, s, d = h.shape
    tm = min(tm, s)
    nt, tf = cw.shape[0], cw.shape[3] // 2
    ns = d // LANES
    final = gf is not None
    in_specs = _slab_specs(tm, d) + [_const_spec((1, d)), _const_spec(wup.shape),
                                     _const_spec(cw.shape), _const_spec(wdn.shape)]
    args = [h] * ns + [g, wup, cw, wdn]
    if final:
        in_specs.append(_const_spec((1, d)))
        args.append(gf)
    return pl.pallas_call(
        functools.partial(_ffn_kernel, tm=tm, tf=tf, nt=nt, dk=dk, d=d, final=final),
        grid=(b, s // tm),
        in_specs=in_specs,
        out_specs=pl.BlockSpec((None, tm, d), lambda bi, ti: (bi, ti, 0)),
        out_shape=jax.ShapeDtypeStruct((b, s, d), F32),
        scratch_shapes=[
            pltpu.VMEM((tm, d), BF16),
            pltpu.VMEM((HALO, nt * 2 * tf), F32),
            pltpu.VMEM((2, tm // SUB_ROWS * WORK_SUB, 2 * tf), F32),
            pltpu.VMEM((2, tm, dk * tf), BF16),
            pltpu.VMEM((ns, tm, LANES), F32),
        ],
        compiler_params=pltpu.CompilerParams(
            dimension_semantics=("arbitrary", "arbitrary"),
            vmem_limit_bytes=VMEM_LIMIT),
        name="conv_ffn_final" if final else "conv_ffn",
    )(*args)


def _shortconv_kernel(*refs, tm, tf, nt, dk, d):
    ns = d // LANES
    x_refs, refs = refs[:ns], refs[ns:]
    (g_ref, win_ref, cw_ref, wout_ref, o_ref,
     xn_ref, carry_ref, work_ref, bg_ref, a_ref, acc_ref) = refs

    @pl.when(pl.program_id(1) == 0)
    def _():
        carry_ref[...] = jnp.zeros_like(carry_ref)

    _load_rows_permuted(x_refs, g_ref[...], xn_ref)

    def up(j):
        bg, cg, hx = [jnp.dot(xn_ref[...], win_ref[:, part * d + j * tf:part * d + (j + 1) * tf],
                              preferred_element_type=F32) for part in range(3)]
        bg_ref[j % 2] = bg
        _store_with_halo(work_ref.at[j % 2], carry_ref, j * tf, cg * hx)

    def gate(j):
        work, cw = work_ref.at[j % 2], cw_ref.at[j]
        for r in range(0, tm, ROW_CHUNK):
            c = _conv_rows(work, cw, r // SUB_ROWS, r % SUB_ROWS, ROW_CHUNK)
            a_ref[(j // dk) % 2, r:r + ROW_CHUNK, (j % dk) * tf:(j % dk + 1) * tf] = (
                bg_ref[j % 2, r:r + ROW_CHUNK, :] * c).astype(BF16)

    down = functools.partial(_down_group, a_ref, wout_ref, acc_ref, nt=nt, dk=dk, tf=tf)
    _run_tiles(nt, dk, up, gate, down)

    for t in range(tm // SUBLANES):
        rows = slice(t * SUBLANES, (t + 1) * SUBLANES)
        o_ref[rows, :] = (jnp.concatenate([xr[rows, :] for xr in x_refs], axis=-1)
                          + _natural_tile(acc_ref, t))


def _shortconv_call(h, g, win, cw, wout, *, tm=CONV_ROWS, dk=MIX_GROUP):
    b, s, d = h.shape
    tm = min(tm, s)
    nt, tf = cw.shape[0], cw.shape[3]
    ns = d // LANES
    return pl.pallas_call(
        functools.partial(_shortconv_kernel, tm=tm, tf=tf, nt=nt, dk=dk, d=d),
        grid=(b, s // tm),
        in_specs=_slab_specs(tm, d) + [_const_spec((1, d)), _const_spec(win.shape),
                                       _const_spec(cw.shape), _const_spec(wout.shape)],
        out_specs=pl.BlockSpec((None, tm, d), lambda bi, ti: (bi, ti, 0)),
        out_shape=jax.ShapeDtypeStruct((b, s, d), F32),
        scratch_shapes=[
            pltpu.VMEM((tm, d), BF16),
            pltpu.VMEM((HALO, nt * tf), F32),
            pltpu.VMEM((2, tm // SUB_ROWS * WORK_SUB, tf), F32),
            pltpu.VMEM((2, tm, tf), F32),
            pltpu.VMEM((2, tm, dk * tf), BF16),
            pltpu.VMEM((ns, tm, LANES), F32),
        ],
        compiler_params=pltpu.CompilerParams(
            dimension_semantics=("arbitrary", "arbitrary"),
            vmem_limit_bytes=VMEM_LIMIT),
        name="shortconv_mixer",
    )(*([h] * ns), g, win, cw, wout)


def _s5_prep_kernel(lr_ref, li_ref, ldt_ref, br_ref, bi_ref, ar_ref, ai_ref, bbr_ref, bbi_ref):
    lr = jnp.minimum(lr_ref[...], LAMBDA_RE_MAX)
    li = li_ref[...]
    dt = jnp.exp(ldt_ref[...])
    mag = jnp.exp(lr * dt)
    ab_re = mag * jnp.cos(li * dt)
    ab_im = mag * jnp.sin(li * dt)
    den = lr * lr + li * li
    nr = ab_re - 1.0
    ni = ab_im
    z_re = (nr * lr + ni * li) / den
    z_im = (ni * lr - nr * li) / den
    br = br_ref[...]
    bi = bi_ref[...]
    ar_ref[...] = ab_re
    ai_ref[...] = ab_im
    bbr_ref[...] = z_re * br - z_im * bi
    bbi_ref[...] = z_re * bi + z_im * br


def _s5_prep(lam_re, lam_im, log_dt, b_re, b_im):
    g, p = lam_re.shape
    n = g * p
    col = lambda a: a.reshape(n, 1)
    ldt = jnp.broadcast_to(log_dt[:, None], (g, p))
    outs = pl.pallas_call(
        _s5_prep_kernel,
        out_shape=[jax.ShapeDtypeStruct((n, 1), F32), jax.ShapeDtypeStruct((n, 1), F32),
                   jax.ShapeDtypeStruct((n, SSM_GROUP), F32),
                   jax.ShapeDtypeStruct((n, SSM_GROUP), F32)],
        name="s5_discretise",
    )(col(lam_re), col(lam_im), col(ldt), b_re.reshape(n, SSM_GROUP), b_im.reshape(n, SSM_GROUP))
    return outs


def _mix0_kernel(x_ref, g_ref, win_ref, bblk_ref, ar_ref, ai_ref, cblk_ref, dsk_ref,
                 wglu_ref, bglu_ref, ws_ref, bs_ref, gv_ref, wout_ref, o_ref,
                 xn_ref, up_ref, utb_ref, xs_ref, hst_ref, ytb_ref, vn_ref, gm_ref, s5_ref, acc_ref,
                 *, nb, tc, pitch, d, ssm_w, gm_w):
    rows = nb * tc
    nslab = xs_ref.shape[0]
    npair = nslab // 2
    uslab = ssm_w // LANES
    heads = gm_w // GMLP_HEAD

    @pl.when(pl.program_id(1) == 0)
    def _():
        hst_ref[...] = jnp.zeros_like(hst_ref)

    x = x_ref[...].reshape(rows, d)
    xn_ref[...] = _rms(x, g_ref[...]).astype(BF16)

    half = rows // 2
    us = [jnp.dot(xn_ref[r0:r0 + half, :], win_ref[:, 0:ssm_w], preferred_element_type=F32)
          for r0 in (0, half)]
    for b in range(nb):
        u, r0 = us[b * tc // half], b * tc % half
        for c in range(uslab):
            up_ref[c, b * pitch:b * pitch + tc, :] = u[r0:r0 + tc, c * LANES:(c + 1) * LANES]
    for t in range(tc):
        utb_ref[t * nb:(t + 1) * nb, :] = jnp.concatenate(
            [up_ref[c, pl.ds(t, nb, stride=pitch), :] for c in range(uslab)], axis=-1)

    ub = utb_ref[...].astype(BF16)
    for c in range(npair):
        xs = jnp.dot(ub, bblk_ref[:, c * 2 * LANES:(c + 1) * 2 * LANES], preferred_element_type=F32)
        xs_ref[2 * c] = xs[:, 0:LANES]
        xs_ref[2 * c + 1] = xs[:, LANES:2 * LANES]

    v0 = ssm_w + gm_w
    vv = _gelu(jnp.dot(xn_ref[...], win_ref[:, v0:v0 + gm_w], preferred_element_type=F32))
    vn_ref[...] = _rms(vv, gv_ref[...]).astype(BF16)

    ar = [jnp.broadcast_to(ar_ref[c], (nb, LANES)) for c in range(npair)]
    ai = [jnp.broadcast_to(ai_ref[c], (nb, LANES)) for c in range(npair)]
    h = [hst_ref[k] for k in range(nslab)]
    for t in range(tc):
        rt = slice(t * nb, (t + 1) * nb)
        for c in range(npair):
            hr, hi = h[2 * c], h[2 * c + 1]
            nr = ar[c] * hr - ai[c] * hi + xs_ref[2 * c, rt, :]
            ni = ar[c] * hi + ai[c] * hr + xs_ref[2 * c + 1, rt, :]
            xs_ref[2 * c, rt, :] = nr
            xs_ref[2 * c + 1, rt, :] = ni
            h[2 * c], h[2 * c + 1] = nr, ni
    for k in range(nslab):
        hst_ref[k] = h[k]

    rr = lax.broadcasted_iota(jnp.int32, (CHUNK, CHUNK), 0)
    cc = lax.broadcasted_iota(jnp.int32, (CHUNK, CHUNK), 1)
    tril = rr >= cc
    blocks = [b * tc + n * CHUNK for b in range(nb) for n in range(tc // CHUNK)]
    uus = [_gelu(jnp.dot(xn_ref[...], win_ref[:, ssm_w + hp * 2 * GMLP_HEAD:ssm_w + (hp + 1) * 2 * GMLP_HEAD],
                         preferred_element_type=F32)) for hp in range(heads // 2)]
    for hh in range(heads):
        uu, k = uus[hh // 2], hh % 2
        w = jnp.where(tril, ws_ref[hh], 0.0).astype(BF16)
        vcols = slice(hh * GMLP_HEAD, (hh + 1) * GMLP_HEAD)
        rhs = jnp.concatenate([vn_ref[r0:r0 + CHUNK, vcols] for r0 in blocks], axis=1)
        gate = jnp.dot(w, rhs, preferred_element_type=F32) + bs_ref[hh]
        for i, r0 in enumerate(blocks):
            gm_ref[r0:r0 + CHUNK, vcols] = (
                uu[r0:r0 + CHUNK, k * GMLP_HEAD:(k + 1) * GMLP_HEAD]
                * gate[:, i * GMLP_HEAD:(i + 1) * GMLP_HEAD]).astype(BF16)

    acc_ref[...] = x + jnp.dot(gm_ref[...], wout_ref[ssm_w:, :], preferred_element_type=F32)

    halves = [slice(r0, r0 + half) for r0 in (0, half)]
    ys = [jnp.dot(jnp.concatenate([xs_ref[k, rs, :] for k in range(nslab)], axis=-1).astype(BF16),
                  cblk_ref[...], preferred_element_type=F32) for rs in halves]
    ys = [_gelu(y + dsk_ref[...] * utb_ref[rs, :]) for y, rs in zip(ys, halves)]
    gls = [jnp.dot(y.astype(BF16), wglu_ref[...], preferred_element_type=F32) + bglu_ref[...]
           for y in ys]
    for y, gl, rs in zip(ys, gls, halves):
        y = y * _sigmoid(gl)
        for c in range(uslab):
            ytb_ref[c, rs, :] = y[:, c * LANES:(c + 1) * LANES]
    for b in range(nb):
        for t0 in range(0, tc, 2 * SUBLANES):
            tile = jnp.concatenate(
                [jnp.concatenate([ytb_ref[c, pl.ds((t0 + dt) * nb + b, SUBLANES, stride=nb), :]
                                  for c in range(uslab)], axis=-1) for dt in (0, SUBLANES)], axis=0)
            s5_ref[b * tc + t0:b * tc + t0 + 2 * SUBLANES, :] = tile.astype(BF16)

    out = acc_ref[...] + jnp.dot(s5_ref[...], wout_ref[0:ssm_w, :], preferred_element_type=F32)
    o_ref[...] = out.reshape(nb, tc, d)


def _mix0_call(h, g, win, bblk, ar, ai, cblk, dsk, wglu, bglu, ws, bs, gv, wout, *, nb=8, tc=128):
    b, s, d = h.shape
    ssm_w = wglu.shape[0]
    gm_w = gv.shape[1]
    nslab = bblk.shape[1] // LANES
    pitch = tc + SUBLANES
    rows = nb * tc
    kern = functools.partial(_mix0_kernel, nb=nb, tc=tc, pitch=pitch, d=d, ssm_w=ssm_w, gm_w=gm_w)
    return pl.pallas_call(
        kern,
        grid=(b // nb, s // tc),
        in_specs=[
            pl.BlockSpec((nb, tc, d), lambda bi, ti: (bi, ti, 0)),
            _const_spec((1, d)),
            _const_spec(win.shape),
            _const_spec(bblk.shape),
            _const_spec(ar.shape),
            _const_spec(ai.shape),
            _const_spec(cblk.shape),
            _const_spec((1, ssm_w)),
            _const_spec(wglu.shape),
            _const_spec((1, ssm_w)),
            _const_spec(ws.shape),
            _const_spec(bs.shape),
            _const_spec((1, gm_w)),
            _const_spec(wout.shape),
        ],
        out_specs=pl.BlockSpec((nb, tc, d), lambda bi, ti: (bi, ti, 0)),
        out_shape=jax.ShapeDtypeStruct((b, s, d), F32),
        scratch_shapes=[
            pltpu.VMEM((rows, d), BF16),
            pltpu.VMEM((ssm_w // LANES, nb * pitch, LANES), F32),
            pltpu.VMEM((rows, ssm_w), F32),
            pltpu.VMEM((nslab, rows, LANES), F32),
            pltpu.VMEM((nslab, nb, LANES), F32),
            pltpu.VMEM((ssm_w // LANES, rows, LANES), F32),
            pltpu.VMEM((rows, gm_w), BF16),
            pltpu.VMEM((rows, gm_w), BF16),
            pltpu.VMEM((rows, ssm_w), BF16),
            pltpu.VMEM((rows, d), F32),
        ],
        compiler_params=pltpu.CompilerParams(
            dimension_semantics=("arbitrary", "arbitrary"),
            vmem_limit_bytes=VMEM_LIMIT),
        name="s5_gmlp_mixer",
    )(h, g, win, bblk, ar, ai, cblk, dsk, wglu, bglu, ws, bs, gv, wout)


def _s5_operands(lam_re, lam_im, log_dt, b_re, b_im, c_re, c_im):
    g, p = lam_re.shape
    hdim = b_re.shape[-1]
    n = g * p
    npair = n // LANES
    ab_re, ab_im, bb_re, bb_im = _s5_prep(lam_re, lam_im, log_dt, b_re, b_im)
    eye = jnp.eye(g, dtype=F32)

    def b_dense(bb):
        bb = bb.reshape(g, p, hdim)
        return jnp.einsum('gph,gk->ghkp', bb, eye).reshape(g * hdim, npair, LANES)

    def c_dense(cm):
        return jnp.einsum('ghp,gk->gpkh', cm, eye).reshape(npair, LANES, g * hdim)

    bblk = jnp.stack([b_dense(bb_re), b_dense(bb_im)], axis=2).reshape(g * hdim, 2 * n)
    cblk = jnp.stack([c_dense(c_re), -c_dense(c_im)], axis=1).reshape(2 * n, g * hdim)
    ar = ab_re.reshape(npair, 1, LANES)
    ai = ab_im.reshape(npair, 1, LANES)
    return bblk.astype(BF16), ar, ai, cblk.astype(BF16)


def kernel(x, mix_norm_g, ffn_norm_g, final_norm_g, ev_w_in, ev_w_out, s5_lam_re, s5_lam_im,
           s5_log_dt, s5_b_re, s5_b_im, s5_c_re, s5_c_im, s5_d, s5_w_glu, s5_b_glu, gm_w_s,
           gm_b_s, gm_v_g, od_w_in, od_conv_w, od_conv_b, od_w_out, ffn_w_up, ffn_conv_w,
           ffn_conv_b, ffn_w_down):
    depth = mix_norm_g.shape[0]
    row = lambda a: a.reshape(1, -1).astype(F32)

    def col_tiles(w, parts, tf):
        k, n = w.shape[0], w.shape[1] // parts
        return w.reshape(k, parts, n // tf, tf).transpose(2, 0, 1, 3).reshape(n // tf, k, parts * tf)

    def conv_tiles(cw, cb, parts, tf):
        t = col_tiles(jnp.concatenate([cw, cb[None, :]], axis=0).astype(F32), parts, tf)
        return jnp.broadcast_to(t[:, :, None, :], t.shape[:2] + (SUBLANES, t.shape[2]))

    h = x
    for layer in range(depth):
        if layer % 2 == 0:
            e = layer // 2
            bblk, ar, ai, cblk = _s5_operands(s5_lam_re[e], s5_lam_im[e], s5_log_dt[e],
                                              s5_b_re[e], s5_b_im[e], s5_c_re[e], s5_c_im[e])
            h = _mix0_call(h, row(mix_norm_g[layer]), ev_w_in[e].astype(BF16), bblk, ar, ai, cblk,
                           row(s5_d[e]), s5_w_glu[e].astype(BF16), row(s5_b_glu[e]),
                           gm_w_s[e], gm_b_s[e][:, :, None], row(gm_v_g[e]),
                           ev_w_out[e].astype(BF16))
        else:
            o = layer // 2
            h = _shortconv_call(h, row(mix_norm_g[layer]),
                                od_w_in[o].astype(BF16),
                                conv_tiles(od_conv_w[o], od_conv_b[o], 1, MIX_TILE),
                                od_w_out[o].astype(BF16))
        gf = row(final_norm_g) if layer == depth - 1 else None
        h = _ffn_call(h, row(ffn_norm_g[layer]),
                      ffn_w_up[layer].astype(BF16),
                      conv_tiles(ffn_conv_w[layer], ffn_conv_b[layer], 2, FFN_TILE),
                      ffn_w_down[layer].astype(BF16), gf)
    return h
```

```python
import functools
import math

import jax
import jax.numpy as jnp
from jax import lax
from jax.experimental import pallas as pl
from jax.experimental.pallas import tpu as pltpu

F32 = jnp.float32
BF16 = jnp.bfloat16

EPS = 1e-6
LAMBDA_RE_MAX = -1e-4
LANES = 128
SUBLANES = 8
HALO = 2 * SUBLANES
SUB_ROWS = SUBLANES * SUBLANES
WORK_SUB = HALO + SUB_ROWS
ROW_CHUNK = 32
CONV_ROWS = 1024
FFN_TILE = 256
FFN_GROUP = 4
MIX_TILE = 256
MIX_GROUP = 2
SSM_GROUP = 16
SSM_STATE = 64
GMLP_HEAD = 128
CHUNK = 128
VMEM_LIMIT = 56 * 1024 * 1024


def _rms(x, g):
    ms = jnp.mean(x * x, axis=-1, keepdims=True)
    return x * lax.rsqrt(ms + EPS) * g


def _gelu(x):
    c = math.sqrt(2.0 / math.pi)
    return 0.5 * x * (1.0 + jnp.tanh(c * (x + 0.044715 * (x * x * x))))


def _sigmoid(x):
    return 1.0 / (1.0 + jnp.exp(-x))


def _const_spec(shape):
    nd = len(shape)
    return pl.BlockSpec(shape, lambda *_: (0,) * nd, pipeline_mode=pl.Buffered(1))


def _transposed_tile(slab_ref, sb, k):
    return slab_ref[pl.ds(sb * SUB_ROWS + k, SUBLANES, stride=SUBLANES), :]


def _load_rows_permuted(x_refs, g, xn_ref):
    for t in range(0, xn_ref.shape[0] // SUBLANES, 2):
        xp = jnp.concatenate(
            [jnp.concatenate([_transposed_tile(xr, tt // SUBLANES, tt % SUBLANES) for xr in x_refs],
                             axis=-1) for tt in (t, t + 1)], axis=0)
        xn_ref[t * SUBLANES:(t + 2) * SUBLANES, :] = _rms(xp, g).astype(BF16)


def _store_with_halo(work, carry_ref, col0, hp):
    tm, w = hp.shape
    sub = lax.broadcasted_iota(jnp.int32, (SUBLANES, w), 0)
    for sb in range(tm // SUB_ROWS):
        base = sb * WORK_SUB
        work[base + HALO:base + WORK_SUB, :] = hp[sb * SUB_ROWS:(sb + 1) * SUB_ROWS, :]
        for k in range(2):
            tail = work[base + WORK_SUB - HALO + k * SUBLANES:base + WORK_SUB - HALO + (k + 1) * SUBLANES, :]
            if sb == 0:
                prev = carry_ref[k * SUBLANES:(k + 1) * SUBLANES, col0:col0 + w]
            else:
                prev = work[base - HALO + k * SUBLANES:base - HALO + (k + 1) * SUBLANES, :]
            work[base + k * SUBLANES:base + (k + 1) * SUBLANES, :] = pltpu.roll(
                jnp.where(sub == SUBLANES - 1, prev, tail), 1, 0)
    last = (tm // SUB_ROWS) * WORK_SUB
    carry_ref[:, col0:col0 + w] = work[last - HALO:last, :]


def _conv_rows(work, cw, sb, r, n):
    s = SUBLANES
    r0 = sb * WORK_SUB + r
    tiles = lambda a: a.reshape(n // s, s, a.shape[-1])
    c = (cw[3] + cw[0] * tiles(work[r0:r0 + n, :]) + cw[1] * tiles(work[s + r0:s + r0 + n, :])
         + cw[2] * tiles(work[2 * s + r0:2 * s + r0 + n, :]))
    return c.reshape(n, c.shape[-1])


def _acc_slabs(acc_ref, dd, first):
    for c in range(acc_ref.shape[0]):
        if first:
            acc_ref[c] = dd[:, c * LANES:(c + 1) * LANES]
        else:
            acc_ref[c] += dd[:, c * LANES:(c + 1) * LANES]


def _natural_tile(acc_ref, t):
    return jnp.concatenate([_transposed_tile(acc_ref.at[c], t // SUBLANES, t % SUBLANES)
                            for c in range(acc_ref.shape[0])], axis=-1)


def _run_tiles(nt, dk, up, gate, down):
    up(0)
    for j in range(nt):
        if j + 1 < nt:
            up(j + 1)
        if j > 0 and j % dk == 0:
            down(j // dk - 1)
        gate(j)
    down((nt - 1) // dk)


def _down_group(a_ref, w_ref, acc_ref, g, nt, dk, tf):
    n = min(dk, nt - g * dk)
    dd = jnp.dot(a_ref[g % 2, :, 0:n * tf], w_ref[g * dk * tf:(g * dk + n) * tf, :],
                 preferred_element_type=F32)
    _acc_slabs(acc_ref, dd, first=(g == 0))


def _ffn_kernel(*refs, tm, tf, nt, dk, d, final):
    ns = d // LANES
    x_refs, refs = refs[:ns], refs[ns:]
    if final:
        (g_ref, wup_ref, cw_ref, wdn_ref, gf_ref, o_ref,
         xn_ref, carry_ref, work_ref, a_ref, acc_ref) = refs
    else:
        (g_ref, wup_ref, cw_ref, wdn_ref, o_ref,
         xn_ref, carry_ref, work_ref, a_ref, acc_ref) = refs

    @pl.when(pl.program_id(1) == 0)
    def _():
        carry_ref[...] = jnp.zeros_like(carry_ref)

    _load_rows_permuted(x_refs, g_ref[...], xn_ref)

    def up(j):
        dff = nt * tf
        for part in range(2):
            c0 = part * dff + j * tf
            hp = jnp.dot(xn_ref[...], wup_ref[:, c0:c0 + tf], preferred_element_type=F32)
            _store_with_halo(work_ref.at[j % 2, :, part * tf:(part + 1) * tf], carry_ref, c0, hp)

    def gate(j):
        work, cw = work_ref.at[j % 2], cw_ref.at[j]
        for r in range(0, tm, ROW_CHUNK):
            c = _conv_rows(work, cw, r // SUB_ROWS, r % SUB_ROWS, ROW_CHUNK)
            gt, vl = c[:, :tf], c[:, tf:]
            a_ref[(j // dk) % 2, r:r + ROW_CHUNK, (j % dk) * tf:(j % dk + 1) * tf] = (
                gt * _sigmoid(gt) * vl).astype(BF16)

    down = functools.partial(_down_group, a_ref, wdn_ref, acc_ref, nt=nt, dk=dk, tf=tf)
    _run_tiles(nt, dk, up, gate, down)

    for t in range(tm // SUBLANES):
        rows = slice(t * SUBLANES, (t + 1) * SUBLANES)
        out = jnp.concatenate([xr[rows, :] for xr in x_refs], axis=-1) + _natural_tile(acc_ref, t)
        if final:
            out = _rms(out, gf_ref[...])
        o_ref[rows, :] = out


def _slab_specs(tm, d):
    return [pl.BlockSpec((None, tm, LANES), lambda bi, ti, c=c: (bi, ti, c)) for c in range(d // LANES)]


def _layer_spec(w, layer):
    return pl.BlockSpec((None,) + w.shape[1:], lambda *_: (layer, 0, 0), pipeline_mode=pl.Buffered(1))


def _ffn_call(h, g, wup, cw, wdn, layer, gf=None, *, tm=CONV_ROWS, dk=FFN_GROUP):
    b, s, d = h.shape
    tm = min(tm, s)
    nt, tf = cw.shape[0], cw.shape[3] // 2
    ns = d // LANES
    final = gf is not None
    in_specs = _slab_specs(tm, d) + [_const_spec((1, d)), _layer_spec(wup, layer),
                                     _const_spec(cw.shape), _layer_spec(wdn, layer)]
    args = [h] * ns + [g, wup, cw, wdn]
    if final:
        in_specs.append(_const_spec((1, d)))
        args.append(gf)
    return pl.pallas_call(
        functools.partial(_ffn_kernel, tm=tm, tf=tf, nt=nt, dk=dk, d=d, final=final),
        grid=(b, s // tm),
        in_specs=in_specs,
        out_specs=pl.BlockSpec((None, tm, d), lambda bi, ti: (bi, ti, 0)),
        out_shape=jax.ShapeDtypeStruct((b, s, d), F32),
        scratch_shapes=[
            pltpu.VMEM((tm, d), BF16),
            pltpu.VMEM((HALO, nt * 2 * tf), F32),
            pltpu.VMEM((2, tm // SUB_ROWS * WORK_SUB, 2 * tf), F32),
            pltpu.VMEM((2, tm, dk * tf), BF16),
            pltpu.VMEM((ns, tm, LANES), F32),
        ],
        compiler_params=pltpu.CompilerParams(
            dimension_semantics=("arbitrary", "arbitrary"),
            vmem_limit_bytes=VMEM_LIMIT),
        name="conv_ffn_final" if final else "conv_ffn",
    )(*args)


def _shortconv_kernel(*refs, tm, tf, nt, dk, d):
    ns = d // LANES
    x_refs, refs = refs[:ns], refs[ns:]
    (g_ref, win_ref, cw_ref, wout_ref, o_ref,
     xn_ref, carry_ref, work_ref, bg_ref, a_ref, acc_ref) = refs

    @pl.when(pl.program_id(1) == 0)
    def _():
        carry_ref[...] = jnp.zeros_like(carry_ref)

    _load_rows_permuted(x_refs, g_ref[...], xn_ref)

    def up(j):
        bg, cg, hx = [jnp.dot(xn_ref[...], win_ref[:, part * d + j * tf:part * d + (j + 1) * tf],
                              preferred_element_type=F32) for part in range(3)]
        bg_ref[j % 2] = bg
        _store_with_halo(work_ref.at[j % 2], carry_ref, j * tf, cg * hx)

    def gate(j):
        work, cw = work_ref.at[j % 2], cw_ref.at[j]
        for r in range(0, tm, ROW_CHUNK):
            c = _conv_rows(work, cw, r // SUB_ROWS, r % SUB_ROWS, ROW_CHUNK)
            a_ref[(j // dk) % 2, r:r + ROW_CHUNK, (j % dk) * tf:(j % dk + 1) * tf] = (
                bg_ref[j % 2, r:r + ROW_CHUNK, :] * c).astype(BF16)

    down = functools.partial(_down_group, a_ref, wout_ref, acc_ref, nt=nt, dk=dk, tf=tf)
    _run_tiles(nt, dk, up, gate, down)

    for t in range(tm // SUBLANES):
        rows = slice(t * SUBLANES, (t + 1) * SUBLANES)
        o_ref[rows, :] = (jnp.concatenate([xr[rows, :] for xr in x_refs], axis=-1)
                          + _natural_tile(acc_ref, t))


def _shortconv_call(h, g, win, cw, wout, *, tm=CONV_ROWS, dk=MIX_GROUP):
    b, s, d = h.shape
    tm = min(tm, s)
    nt, tf = cw.shape[0], cw.shape[3]
    ns = d // LANES
    return pl.pallas_call(
        functools.partial(_shortconv_kernel, tm=tm, tf=tf, nt=nt, dk=dk, d=d),
        grid=(b, s // tm),
        in_specs=_slab_specs(tm, d) + [_const_spec((1, d)), _const_spec(win.shape),
                                       _const_spec(cw.shape), _const_spec(wout.shape)],
        out_specs=pl.BlockSpec((None, tm, d), lambda bi, ti: (bi, ti, 0)),
        out_shape=jax.ShapeDtypeStruct((b, s, d), F32),
        scratch_shapes=[
            pltpu.VMEM((tm, d), BF16),
            pltpu.VMEM((HALO, nt * tf), F32),
            pltpu.VMEM((2, tm // SUB_ROWS * WORK_SUB, tf), F32),
            pltpu.VMEM((2, tm, tf), F32),
            pltpu.VMEM((2, tm, dk * tf), BF16),
            pltpu.VMEM((ns, tm, LANES), F32),
        ],
        compiler_params=pltpu.CompilerParams(
            dimension_semantics=("arbitrary", "arbitrary"),
            vmem_limit_bytes=VMEM_LIMIT),
        name="shortconv_mixer",
    )(*([h] * ns), g, win, cw, wout)


def _s5_prep_kernel(lr_ref, li_ref, ldt_ref, br_ref, bi_ref, ar_ref, ai_ref, bbr_ref, bbi_ref):
    lr = jnp.minimum(lr_ref[...], LAMBDA_RE_MAX)
    li = li_ref[...]
    dt = jnp.exp(ldt_ref[...])
    mag = jnp.exp(lr * dt)
    ab_re = mag * jnp.cos(li * dt)
    ab_im = mag * jnp.sin(li * dt)
    den = lr * lr + li * li
    nr = ab_re - 1.0
    ni = ab_im
    z_re = (nr * lr + ni * li) / den
    z_im = (ni * lr - nr * li) / den
    br = br_ref[...]
    bi = bi_ref[...]
    ar_ref[...] = ab_re
    ai_ref[...] = ab_im
    bbr_ref[...] = z_re * br - z_im * bi
    bbi_ref[...] = z_re * bi + z_im * br


def _s5_prep(lam_re, lam_im, log_dt, b_re, b_im):
    g, p = lam_re.shape
    n = g * p
    col = lambda a: a.reshape(n, 1)
    ldt = jnp.broadcast_to(log_dt[:, None], (g, p))
    outs = pl.pallas_call(
        _s5_prep_kernel,
        out_shape=[jax.ShapeDtypeStruct((n, 1), F32), jax.ShapeDtypeStruct((n, 1), F32),
                   jax.ShapeDtypeStruct((n, SSM_GROUP), F32),
                   jax.ShapeDtypeStruct((n, SSM_GROUP), F32)],
        name="s5_discretise",
    )(col(lam_re), col(lam_im), col(ldt), b_re.reshape(n, SSM_GROUP), b_im.reshape(n, SSM_GROUP))
    return outs


def _mix0_kernel(x_ref, g_ref, win_ref, bblk_ref, ar_ref, ai_ref, cblk_ref, dsk_ref,
                 wglu_ref, bglu_ref, ws_ref, bs_ref, gv_ref, wout_ref, o_ref,
                 xn_ref, up_ref, utb_ref, xs_ref, hst_ref, ytb_ref, vn_ref, gm_ref, s5_ref, acc_ref,
                 *, nb, tc, pitch, d, ssm_w, gm_w):
    rows = nb * tc
    nslab = xs_ref.shape[0]
    npair = nslab // 2
    uslab = ssm_w // LANES
    heads = gm_w // GMLP_HEAD

    @pl.when(pl.program_id(1) == 0)
    def _():
        hst_ref[...] = jnp.zeros_like(hst_ref)

    x = x_ref[...].reshape(rows, d)
    xn_ref[...] = _rms(x, g_ref[...]).astype(BF16)

    half = rows // 2
    us = [jnp.dot(xn_ref[r0:r0 + half, :], win_ref[:, 0:ssm_w], preferred_element_type=F32)
          for r0 in (0, half)]
    for b in range(nb):
        u, r0 = us[b * tc // half], b * tc % half
        for c in range(uslab):
            up_ref[c, b * pitch:b * pitch + tc, :] = u[r0:r0 + tc, c * LANES:(c + 1) * LANES]
    for t in range(tc):
        utb_ref[t * nb:(t + 1) * nb, :] = jnp.concatenate(
            [up_ref[c, pl.ds(t, nb, stride=pitch), :] for c in range(uslab)], axis=-1)

    ub = utb_ref[...].astype(BF16)
    for c in range(npair):
        xs = jnp.dot(ub, bblk_ref[:, c * 2 * LANES:(c + 1) * 2 * LANES], preferred_element_type=F32)
        xs_ref[2 * c] = xs[:, 0:LANES]
        xs_ref[2 * c + 1] = xs[:, LANES:2 * LANES]

    v0 = ssm_w + gm_w
    vv = _gelu(jnp.dot(xn_ref[...], win_ref[:, v0:v0 + gm_w], preferred_element_type=F32))
    vn_ref[...] = _rms(vv, gv_ref[...]).astype(BF16)

    ar = [jnp.broadcast_to(ar_ref[c], (nb, LANES)) for c in range(npair)]
    ai = [jnp.broadcast_to(ai_ref[c], (nb, LANES)) for c in range(npair)]
    h = [hst_ref[k] for k in range(nslab)]
    for t in range(tc):
        rt = slice(t * nb, (t + 1) * nb)
        for c in range(npair):
            hr, hi = h[2 * c], h[2 * c + 1]
            nr = ar[c] * hr - ai[c] * hi + xs_ref[2 * c, rt, :]
            ni = ar[c] * hi + ai[c] * hr + xs_ref[2 * c + 1, rt, :]
            xs_ref[2 * c, rt, :] = nr
            xs_ref[2 * c + 1, rt, :] = ni
            h[2 * c], h[2 * c + 1] = nr, ni
    for k in range(nslab):
        hst_ref[k] = h[k]

    rr = lax.broadcasted_iota(jnp.int32, (CHUNK, CHUNK), 0)
    cc = lax.broadcasted_iota(jnp.int32, (CHUNK, CHUNK), 1)
    tril = rr >= cc
    blocks = [b * tc + n * CHUNK for b in range(nb) for n in range(tc // CHUNK)]
    uus = [_gelu(jnp.dot(xn_ref[...], win_ref[:, ssm_w + hp * 2 * GMLP_HEAD:ssm_w + (hp + 1) * 2 * GMLP_HEAD],
                         preferred_element_type=F32)) for hp in range(heads // 2)]
    for hh in range(heads):
        uu, k = uus[hh // 2], hh % 2
        w = jnp.where(tril, ws_ref[hh], 0.0).astype(BF16)
        vcols = slice(hh * GMLP_HEAD, (hh + 1) * GMLP_HEAD)
        rhs = jnp.concatenate([vn_ref[r0:r0 + CHUNK, vcols] for r0 in blocks], axis=1)
        gate = jnp.dot(w, rhs, preferred_element_type=F32) + bs_ref[hh]
        for i, r0 in enumerate(blocks):
            gm_ref[r0:r0 + CHUNK, vcols] = (
                uu[r0:r0 + CHUNK, k * GMLP_HEAD:(k + 1) * GMLP_HEAD]
                * gate[:, i * GMLP_HEAD:(i + 1) * GMLP_HEAD]).astype(BF16)

    acc_ref[...] = x + jnp.dot(gm_ref[...], wout_ref[ssm_w:, :], preferred_element_type=F32)

    halves = [slice(r0, r0 + half) for r0 in (0, half)]
    ys = [jnp.dot(jnp.concatenate([xs_ref[k, rs, :] for k in range(nslab)], axis=-1).astype(BF16),
                  cblk_ref[...], preferred_element_type=F32) for rs in halves]
    ys = [_gelu(y + dsk_ref[...] * utb_ref[rs, :]) for y, rs in zip(ys, halves)]
    gls = [jnp.dot(y.astype(BF16), wglu_ref[...], preferred_element_type=F32) + bglu_ref[...]
           for y in ys]
    for y, gl, rs in zip(ys, gls, halves):
        y = y * _sigmoid(gl)
        for c in range(uslab):
            ytb_ref[c, rs, :] = y[:, c * LANES:(c + 1) * LANES]
    for b in range(nb):
        for t0 in range(0, tc, 2 * SUBLANES):
            tile = jnp.concatenate(
                [jnp.concatenate([ytb_ref[c, pl.ds((t0 + dt) * nb + b, SUBLANES, stride=nb), :]
                                  for c in range(uslab)], axis=-1) for dt in (0, SUBLANES)], axis=0)
            s5_ref[b * tc + t0:b * tc + t0 + 2 * SUBLANES, :] = tile.astype(BF16)

    out = acc_ref[...] + jnp.dot(s5_ref[...], wout_ref[0:ssm_w, :], preferred_element_type=F32)
    o_ref[...] = out.reshape(nb, tc, d)


def _mix0_call(h, g, win, bblk, ar, ai, cblk, dsk, wglu, bglu, ws, bs, gv, wout, *, nb=8, tc=128):
    b, s, d = h.shape
    ssm_w = wglu.shape[0]
    gm_w = gv.shape[1]
    nslab = bblk.shape[1] // LANES
    pitch = tc + SUBLANES
    rows = nb * tc
    kern = functools.partial(_mix0_kernel, nb=nb, tc=tc, pitch=pitch, d=d, ssm_w=ssm_w, gm_w=gm_w)
    return pl.pallas_call(
        kern,
        grid=(b // nb, s // tc),
        in_specs=[
            pl.BlockSpec((nb, tc, d), lambda bi, ti: (bi, ti, 0)),
            _const_spec((1, d)),
            _const_spec(win.shape),
            _const_spec(bblk.shape),
            _const_spec(ar.shape),
            _const_spec(ai.shape),
            _const_spec(cblk.shape),
            _const_spec((1, ssm_w)),
            _const_spec(wglu.shape),
            _const_spec((1, ssm_w)),
            _const_spec(ws.shape),
            _const_spec(bs.shape),
            _const_spec((1, gm_w)),
            _const_spec(wout.shape),
        ],
        out_specs=pl.BlockSpec((nb, tc, d), lambda bi, ti: (bi, ti, 0)),
        out_shape=jax.ShapeDtypeStruct((b, s, d), F32),
        scratch_shapes=[
            pltpu.VMEM((rows, d), BF16),
            pltpu.VMEM((ssm_w // LANES, nb * pitch, LANES), F32),
            pltpu.VMEM((rows, ssm_w), F32),
            pltpu.VMEM((nslab, rows, LANES), F32),
            pltpu.VMEM((nslab, nb, LANES), F32),
            pltpu.VMEM((ssm_w // LANES, rows, LANES), F32),
            pltpu.VMEM((rows, gm_w), BF16),
            pltpu.VMEM((rows, gm_w), BF16),
            pltpu.VMEM((rows, ssm_w), BF16),
            pltpu.VMEM((rows, d), F32),
        ],
        compiler_params=pltpu.CompilerParams(
            dimension_semantics=("arbitrary", "arbitrary"),
            vmem_limit_bytes=VMEM_LIMIT),
        name="s5_gmlp_mixer",
    )(h, g, win, bblk, ar, ai, cblk, dsk, wglu, bglu, ws, bs, gv, wout)


def _s5_operands(lam_re, lam_im, log_dt, b_re, b_im, c_re, c_im):
    g, p = lam_re.shape
    hdim = b_re.shape[-1]
    n = g * p
    npair = n // LANES
    ab_re, ab_im, bb_re, bb_im = _s5_prep(lam_re, lam_im, log_dt, b_re, b_im)
    eye = jnp.eye(g, dtype=F32)

    def b_dense(bb):
        bb = bb.reshape(g, p, hdim)
        return jnp.einsum('gph,gk->ghkp', bb, eye).reshape(g * hdim, npair, LANES)

    def c_dense(cm):
        return jnp.einsum('ghp,gk->gpkh', cm, eye).reshape(npair, LANES, g * hdim)

    bblk = jnp.stack([b_dense(bb_re), b_dense(bb_im)], axis=2).reshape(g * hdim, 2 * n)
    cblk = jnp.stack([c_dense(c_re), -c_dense(c_im)], axis=1).reshape(2 * n, g * hdim)
    ar = ab_re.reshape(npair, 1, LANES)
    ai = ab_im.reshape(npair, 1, LANES)
    return bblk.astype(BF16), ar, ai, cblk.astype(BF16)


def kernel(x, mix_norm_g, ffn_norm_g, final_norm_g, ev_w_in, ev_w_out, s5_lam_re, s5_lam_im,
           s5_log_dt, s5_b_re, s5_b_im, s5_c_re, s5_c_im, s5_d, s5_w_glu, s5_b_glu, gm_w_s,
           gm_b_s, gm_v_g, od_w_in, od_conv_w, od_conv_b, od_w_out, ffn_w_up, ffn_conv_w,
           ffn_conv_b, ffn_w_down):
    depth = mix_norm_g.shape[0]
    row = lambda a: a.reshape(1, -1).astype(F32)

    def col_tiles(w, parts, tf):
        k, n = w.shape[0], w.shape[1] // parts
        return w.reshape(k, parts, n // tf, tf).transpose(2, 0, 1, 3).reshape(n // tf, k, parts * tf)

    def conv_tiles(cw, cb, parts, tf):
        t = col_tiles(jnp.concatenate([cw, cb[None, :]], axis=0).astype(F32), parts, tf)
        return jnp.broadcast_to(t[:, :, None, :], t.shape[:2] + (SUBLANES, t.shape[2]))

    ffn_up, ffn_down = ffn_w_up.astype(BF16), ffn_w_down.astype(BF16)
    h = x
    for layer in range(depth):
        if layer % 2 == 0:
            e = layer // 2
            bblk, ar, ai, cblk = _s5_operands(s5_lam_re[e], s5_lam_im[e], s5_log_dt[e],
                                              s5_b_re[e], s5_b_im[e], s5_c_re[e], s5_c_im[e])
            h = _mix0_call(h, row(mix_norm_g[layer]), ev_w_in[e].astype(BF16), bblk, ar, ai, cblk,
                           row(s5_d[e]), s5_w_glu[e].astype(BF16), row(s5_b_glu[e]),
                           gm_w_s[e], gm_b_s[e][:, :, None], row(gm_v_g[e]),
                           ev_w_out[e].astype(BF16))
        else:
            o = layer // 2
            h = _shortconv_call(h, row(mix_norm_g[layer]),
                                od_w_in[o].astype(BF16),
                                conv_tiles(od_conv_w[o], od_conv_b[o], 1, MIX_TILE),
                                od_w_out[o].astype(BF16))
        gf = row(final_norm_g) if layer == depth - 1 else None
        h = _ffn_call(h, row(ffn_norm_g[layer]), ffn_up, conv_tiles(ffn_conv_w[layer], ffn_conv_b[layer], 2, FFN_TILE),
                      ffn_down, layer, gf)
    return h
```

```python
import functools
import math

import jax
import jax.numpy as jnp
from jax import lax
from jax.experimental import pallas as pl
from jax.experimental.pallas import tpu as pltpu

F32 = jnp.float32
BF16 = jnp.bfloat16

EPS = 1e-6
LAMBDA_RE_MAX = -1e-4
LANES = 128
SUBLANES = 8
HALO = 2 * SUBLANES
SUB_ROWS = SUBLANES * SUBLANES
WORK_SUB = HALO + SUB_ROWS
ROW_CHUNK = 32
CONV_ROWS = 1024
FFN_TILE = 256
FFN_GROUP = 4
DOWN_LAG = 1
MIX_TILE = 256
MIX_GROUP = 2
SSM_GROUP = 16
SSM_STATE = 64
GMLP_HEAD = 128
CHUNK = 128
VMEM_LIMIT = 56 * 1024 * 1024


def _rms(x, g):
    ms = jnp.mean(x * x, axis=-1, keepdims=True)
    return x * lax.rsqrt(ms + EPS) * g


def _gelu(x):
    c = math.sqrt(2.0 / math.pi)
    return 0.5 * x * (1.0 + jnp.tanh(c * (x + 0.044715 * (x * x * x))))


def _sigmoid(x):
    return 1.0 / (1.0 + jnp.exp(-x))


def _const_spec(shape):
    nd = len(shape)
    return pl.BlockSpec(shape, lambda *_: (0,) * nd, pipeline_mode=pl.Buffered(1))


def _transposed_tile(slab_ref, sb, k):
    return slab_ref[pl.ds(sb * SUB_ROWS + k, SUBLANES, stride=SUBLANES), :]


def _load_rows_permuted(x_refs, g, xn_ref):
    for t in range(0, xn_ref.shape[0] // SUBLANES, 2):
        xp = jnp.concatenate(
            [jnp.concatenate([_transposed_tile(xr, tt // SUBLANES, tt % SUBLANES) for xr in x_refs],
                             axis=-1) for tt in (t, t + 1)], axis=0)
        xn_ref[t * SUBLANES:(t + 2) * SUBLANES, :] = _rms(xp, g).astype(BF16)


def _store_with_halo(work, carry_ref, col0, hp):
    tm, w = hp.shape
    sub = lax.broadcasted_iota(jnp.int32, (SUBLANES, w), 0)
    for sb in range(tm // SUB_ROWS):
        base = sb * WORK_SUB
        work[base + HALO:base + WORK_SUB, :] = hp[sb * SUB_ROWS:(sb + 1) * SUB_ROWS, :]
        for k in range(2):
            tail = work[base + WORK_SUB - HALO + k * SUBLANES:base + WORK_SUB - HALO + (k + 1) * SUBLANES, :]
            if sb == 0:
                prev = carry_ref[k * SUBLANES:(k + 1) * SUBLANES, col0:col0 + w]
            else:
                prev = work[base - HALO + k * SUBLANES:base - HALO + (k + 1) * SUBLANES, :]
            work[base + k * SUBLANES:base + (k + 1) * SUBLANES, :] = pltpu.roll(
                jnp.where(sub == SUBLANES - 1, prev, tail), 1, 0)
    last = (tm // SUB_ROWS) * WORK_SUB
    carry_ref[:, col0:col0 + w] = work[last - HALO:last, :]


def _conv_rows(work, cw, sb, r, n):
    s = SUBLANES
    r0 = sb * WORK_SUB + r
    tiles = lambda a: a.reshape(n // s, s, a.shape[-1])
    c = (cw[3] + cw[0] * tiles(work[r0:r0 + n, :]) + cw[1] * tiles(work[s + r0:s + r0 + n, :])
         + cw[2] * tiles(work[2 * s + r0:2 * s + r0 + n, :]))
    return c.reshape(n, c.shape[-1])


def _acc_slabs(acc_ref, dd, first):
    for c in range(acc_ref.shape[0]):
        if first:
            acc_ref[c] = dd[:, c * LANES:(c + 1) * LANES]
        else:
            acc_ref[c] += dd[:, c * LANES:(c + 1) * LANES]


def _natural_tile(acc_ref, t):
    return jnp.concatenate([_transposed_tile(acc_ref.at[c], t // SUBLANES, t % SUBLANES)
                            for c in range(acc_ref.shape[0])], axis=-1)


def _run_tiles(nt, dk, up, gate, down):
    up(0)
    groups = -(-nt // dk)
    done = 0
    for j in range(nt):
        if j + 1 < nt:
            up(j + 1)
        if done < groups - 1 and (done + 1) * dk + DOWN_LAG <= j:
            down(done)
            done += 1
        gate(j)
    for g in range(done, groups):
        down(g)


def _down_group(a_ref, w_ref, acc_ref, g, nt, dk, tf):
    n = min(dk, nt - g * dk)
    dd = jnp.dot(a_ref[g % 2, :, 0:n * tf], w_ref[g * dk * tf:(g * dk + n) * tf, :],
                 preferred_element_type=F32)
    _acc_slabs(acc_ref, dd, first=(g == 0))


def _ffn_kernel(*refs, tm, tf, nt, dk, d, final):
    ns = d // LANES
    x_refs, refs = refs[:ns], refs[ns:]
    if final:
        (g_ref, wup_ref, cw_ref, wdn_ref, gf_ref, o_ref,
         xn_ref, carry_ref, work_ref, a_ref, acc_ref) = refs
    else:
        (g_ref, wup_ref, cw_ref, wdn_ref, o_ref,
         xn_ref, carry_ref, work_ref, a_ref, acc_ref) = refs

    @pl.when(pl.program_id(1) == 0)
    def _():
        carry_ref[...] = jnp.zeros_like(carry_ref)

    _load_rows_permuted(x_refs, g_ref[...], xn_ref)

    def up(j):
        dff = nt * tf
        for part in range(2):
            c0 = part * dff + j * tf
            hp = jnp.dot(xn_ref[...], wup_ref[:, c0:c0 + tf], preferred_element_type=F32)
            _store_with_halo(work_ref.at[j % 2, :, part * tf:(part + 1) * tf], carry_ref, c0, hp)

    def gate(j):
        work, cw = work_ref.at[j % 2], cw_ref.at[j]
        for r in range(0, tm, ROW_CHUNK):
            c = _conv_rows(work, cw, r // SUB_ROWS, r % SUB_ROWS, ROW_CHUNK)
            gt, vl = c[:, :tf], c[:, tf:]
            a_ref[(j // dk) % 2, r:r + ROW_CHUNK, (j % dk) * tf:(j % dk + 1) * tf] = (
                gt * _sigmoid(gt) * vl).astype(BF16)

    down = functools.partial(_down_group, a_ref, wdn_ref, acc_ref, nt=nt, dk=dk, tf=tf)
    _run_tiles(nt, dk, up, gate, down)

    for t in range(tm // SUBLANES):
        rows = slice(t * SUBLANES, (t + 1) * SUBLANES)
        out = jnp.concatenate([xr[rows, :] for xr in x_refs], axis=-1) + _natural_tile(acc_ref, t)
        if final:
            out = _rms(out, gf_ref[...])
        o_ref[rows, :] = out


def _slab_specs(tm, d):
    return [pl.BlockSpec((None, tm, LANES), lambda bi, ti, c=c: (bi, ti, c)) for c in range(d // LANES)]


def _layer_spec(w, layer):
    return pl.BlockSpec((None,) + w.shape[1:], lambda *_: (layer, 0, 0), pipeline_mode=pl.Buffered(1))


def _ffn_call(h, g, wup, cw, wdn, layer, gf=None, *, tm=CONV_ROWS, dk=FFN_GROUP):
    b, s, d = h.shape
    tm = min(tm, s)
    nt, tf = cw.shape[0], cw.shape[3] // 2
    ns = d // LANES
    final = gf is not None
    in_specs = _slab_specs(tm, d) + [_const_spec((1, d)), _layer_spec(wup, layer),
                                     _const_spec(cw.shape), _layer_spec(wdn, layer)]
    args = [h] * ns + [g, wup, cw, wdn]
    if final:
        in_specs.append(_const_spec((1, d)))
        args.append(gf)
    return pl.pallas_call(
        functools.partial(_ffn_kernel, tm=tm, tf=tf, nt=nt, dk=dk, d=d, final=final),
        grid=(b, s // tm),
        in_specs=in_specs,
        out_specs=pl.BlockSpec((None, tm, d), lambda bi, ti: (bi, ti, 0)),
        out_shape=jax.ShapeDtypeStruct((b, s, d), F32),
        scratch_shapes=[
            pltpu.VMEM((tm, d), BF16),
            pltpu.VMEM((HALO, nt * 2 * tf), F32),
            pltpu.VMEM((2, tm // SUB_ROWS * WORK_SUB, 2 * tf), F32),
            pltpu.VMEM((2, tm, dk * tf), BF16),
            pltpu.VMEM((ns, tm, LANES), F32),
        ],
        compiler_params=pltpu.CompilerParams(
            dimension_semantics=("arbitrary", "arbitrary"),
            vmem_limit_bytes=VMEM_LIMIT),
        name="conv_ffn_final" if final else "conv_ffn",
    )(*args)


def _shortconv_kernel(*refs, tm, tf, nt, dk, d):
    ns = d // LANES
    x_refs, refs = refs[:ns], refs[ns:]
    (g_ref, win_ref, cw_ref, wout_ref, o_ref,
     xn_ref, carry_ref, work_ref, bg_ref, a_ref, acc_ref) = refs

    @pl.when(pl.program_id(1) == 0)
    def _():
        carry_ref[...] = jnp.zeros_like(carry_ref)

    _load_rows_permuted(x_refs, g_ref[...], xn_ref)

    def up(j):
        bg, cg, hx = [jnp.dot(xn_ref[...], win_ref[:, part * d + j * tf:part * d + (j + 1) * tf],
                              preferred_element_type=F32) for part in range(3)]
        bg_ref[j % 2] = bg
        _store_with_halo(work_ref.at[j % 2], carry_ref, j * tf, cg * hx)

    def gate(j):
        work, cw = work_ref.at[j % 2], cw_ref.at[j]
        for r in range(0, tm, ROW_CHUNK):
            c = _conv_rows(work, cw, r // SUB_ROWS, r % SUB_ROWS, ROW_CHUNK)
            a_ref[(j // dk) % 2, r:r + ROW_CHUNK, (j % dk) * tf:(j % dk + 1) * tf] = (
                bg_ref[j % 2, r:r + ROW_CHUNK, :] * c).astype(BF16)

    down = functools.partial(_down_group, a_ref, wout_ref, acc_ref, nt=nt, dk=dk, tf=tf)
    _run_tiles(nt, dk, up, gate, down)

    for t in range(tm // SUBLANES):
        rows = slice(t * SUBLANES, (t + 1) * SUBLANES)
        o_ref[rows, :] = (jnp.concatenate([xr[rows, :] for xr in x_refs], axis=-1)
                          + _natural_tile(acc_ref, t))


def _shortconv_call(h, g, win, cw, wout, *, tm=CONV_ROWS, dk=MIX_GROUP):
    b, s, d = h.shape
    tm = min(tm, s)
    nt, tf = cw.shape[0], cw.shape[3]
    ns = d // LANES
    return pl.pallas_call(
        functools.partial(_shortconv_kernel, tm=tm, tf=tf, nt=nt, dk=dk, d=d),
        grid=(b, s // tm),
        in_specs=_slab_specs(tm, d) + [_const_spec((1, d)), _const_spec(win.shape),
                                       _const_spec(cw.shape), _const_spec(wout.shape)],
        out_specs=pl.BlockSpec((None, tm, d), lambda bi, ti: (bi, ti, 0)),
        out_shape=jax.ShapeDtypeStruct((b, s, d), F32),
        scratch_shapes=[
            pltpu.VMEM((tm, d), BF16),
            pltpu.VMEM((HALO, nt * tf), F32),
            pltpu.VMEM((2, tm // SUB_ROWS * WORK_SUB, tf), F32),
            pltpu.VMEM((2, tm, tf), F32),
            pltpu.VMEM((2, tm, dk * tf), BF16),
            pltpu.VMEM((ns, tm, LANES), F32),
        ],
        compiler_params=pltpu.CompilerParams(
            dimension_semantics=("arbitrary", "arbitrary"),
            vmem_limit_bytes=VMEM_LIMIT),
        name="shortconv_mixer",
    )(*([h] * ns), g, win, cw, wout)


def _s5_prep_kernel(lr_ref, li_ref, ldt_ref, br_ref, bi_ref, ar_ref, ai_ref, bbr_ref, bbi_ref):
    lr = jnp.minimum(lr_ref[...], LAMBDA_RE_MAX)
    li = li_ref[...]
    dt = jnp.exp(ldt_ref[...])
    mag = jnp.exp(lr * dt)
    ab_re = mag * jnp.cos(li * dt)
    ab_im = mag * jnp.sin(li * dt)
    den = lr * lr + li * li
    nr = ab_re - 1.0
    ni = ab_im
    z_re = (nr * lr + ni * li) / den
    z_im = (ni * lr - nr * li) / den
    br = br_ref[...]
    bi = bi_ref[...]
    ar_ref[...] = ab_re
    ai_ref[...] = ab_im
    bbr_ref[...] = z_re * br - z_im * bi
    bbi_ref[...] = z_re * bi + z_im * br


def _s5_prep(lam_re, lam_im, log_dt, b_re, b_im):
    g, p = lam_re.shape
    n = g * p
    col = lambda a: a.reshape(n, 1)
    ldt = jnp.broadcast_to(log_dt[:, None], (g, p))
    outs = pl.pallas_call(
        _s5_prep_kernel,
        out_shape=[jax.ShapeDtypeStruct((n, 1), F32), jax.ShapeDtypeStruct((n, 1), F32),
                   jax.ShapeDtypeStruct((n, SSM_GROUP), F32),
                   jax.ShapeDtypeStruct((n, SSM_GROUP), F32)],
        name="s5_discretise",
    )(col(lam_re), col(lam_im), col(ldt), b_re.reshape(n, SSM_GROUP), b_im.reshape(n, SSM_GROUP))
    return outs


def _mix0_kernel(x_ref, g_ref, win_ref, bblk_ref, ar_ref, ai_ref, cblk_ref, dsk_ref,
                 wglu_ref, bglu_ref, ws_ref, bs_ref, gv_ref, wout_ref, o_ref,
                 xn_ref, up_ref, utb_ref, xs_ref, hst_ref, ytb_ref, vn_ref, gm_ref, s5_ref, acc_ref,
                 *, nb, tc, pitch, d, ssm_w, gm_w):
    rows = nb * tc
    nslab = xs_ref.shape[0]
    npair = nslab // 2
    uslab = ssm_w // LANES
    heads = gm_w // GMLP_HEAD

    @pl.when(pl.program_id(1) == 0)
    def _():
        hst_ref[...] = jnp.zeros_like(hst_ref)

    x = x_ref[...].reshape(rows, d)
    xn_ref[...] = _rms(x, g_ref[...]).astype(BF16)

    half = rows // 2
    us = [jnp.dot(xn_ref[r0:r0 + half, :], win_ref[:, 0:ssm_w], preferred_element_type=F32)
          for r0 in (0, half)]
    for b in range(nb):
        u, r0 = us[b * tc // half], b * tc % half
        for c in range(uslab):
            up_ref[c, b * pitch:b * pitch + tc, :] = u[r0:r0 + tc, c * LANES:(c + 1) * LANES]
    for t in range(tc):
        utb_ref[t * nb:(t + 1) * nb, :] = jnp.concatenate(
            [up_ref[c, pl.ds(t, nb, stride=pitch), :] for c in range(uslab)], axis=-1)

    ub = utb_ref[...].astype(BF16)
    for c in range(npair):
        xs = jnp.dot(ub, bblk_ref[:, c * 2 * LANES:(c + 1) * 2 * LANES], preferred_element_type=F32)
        xs_ref[2 * c] = xs[:, 0:LANES]
        xs_ref[2 * c + 1] = xs[:, LANES:2 * LANES]

    v0 = ssm_w + gm_w
    vv = _gelu(jnp.dot(xn_ref[...], win_ref[:, v0:v0 + gm_w], preferred_element_type=F32))
    vn_ref[...] = _rms(vv, gv_ref[...]).astype(BF16)

    ar = [jnp.broadcast_to(ar_ref[c], (nb, LANES)) for c in range(npair)]
    ai = [jnp.broadcast_to(ai_ref[c], (nb, LANES)) for c in range(npair)]
    h = [hst_ref[k] for k in range(nslab)]
    for t in range(tc):
        rt = slice(t * nb, (t + 1) * nb)
        for c in range(npair):
            hr, hi = h[2 * c], h[2 * c + 1]
            nr = ar[c] * hr - ai[c] * hi + xs_ref[2 * c, rt, :]
            ni = ar[c] * hi + ai[c] * hr + xs_ref[2 * c + 1, rt, :]
            xs_ref[2 * c, rt, :] = nr
            xs_ref[2 * c + 1, rt, :] = ni
            h[2 * c], h[2 * c + 1] = nr, ni
    for k in range(nslab):
        hst_ref[k] = h[k]

    rr = lax.broadcasted_iota(jnp.int32, (CHUNK, CHUNK), 0)
    cc = lax.broadcasted_iota(jnp.int32, (CHUNK, CHUNK), 1)
    tril = rr >= cc
    blocks = [b * tc + n * CHUNK for b in range(nb) for n in range(tc // CHUNK)]
    uus = [_gelu(jnp.dot(xn_ref[...], win_ref[:, ssm_w + hp * 2 * GMLP_HEAD:ssm_w + (hp + 1) * 2 * GMLP_HEAD],
                         preferred_element_type=F32)) for hp in range(heads // 2)]
    for hh in range(heads):
        uu, k = uus[hh // 2], hh % 2
        w = jnp.where(tril, ws_ref[hh], 0.0).astype(BF16)
        vcols = slice(hh * GMLP_HEAD, (hh + 1) * GMLP_HEAD)
        rhs = jnp.concatenate([vn_ref[r0:r0 + CHUNK, vcols] for r0 in blocks], axis=1)
        gate = jnp.dot(w, rhs, preferred_element_type=F32) + bs_ref[hh]
        for i, r0 in enumerate(blocks):
            gm_ref[r0:r0 + CHUNK, vcols] = (
                uu[r0:r0 + CHUNK, k * GMLP_HEAD:(k + 1) * GMLP_HEAD]
                * gate[:, i * GMLP_HEAD:(i + 1) * GMLP_HEAD]).astype(BF16)

    acc_ref[...] = x + jnp.dot(gm_ref[...], wout_ref[ssm_w:, :], preferred_element_type=F32)

    halves = [slice(r0, r0 + half // 2) for r0 in range(0, rows, half // 2)]
    ys = [jnp.dot(jnp.concatenate([xs_ref[k, rs, :] for k in range(nslab)], axis=-1).astype(BF16),
                  cblk_ref[...], preferred_element_type=F32) for rs in halves]
    ys = [_gelu(y + dsk_ref[...] * utb_ref[rs, :]) for y, rs in zip(ys, halves)]
    gls = [jnp.dot(y.astype(BF16), wglu_ref[...], preferred_element_type=F32) + bglu_ref[...]
           for y in ys]
    for y, gl, rs in zip(ys, gls, halves):
        y = y * _sigmoid(gl)
        for c in range(uslab):
            ytb_ref[c, rs, :] = y[:, c * LANES:(c + 1) * LANES]
    for b in range(nb):
        for t0 in range(0, tc, 2 * SUBLANES):
            tile = jnp.concatenate(
                [jnp.concatenate([ytb_ref[c, pl.ds((t0 + dt) * nb + b, SUBLANES, stride=nb), :]
                                  for c in range(uslab)], axis=-1) for dt in (0, SUBLANES)], axis=0)
            s5_ref[b * tc + t0:b * tc + t0 + 2 * SUBLANES, :] = tile.astype(BF16)

    out = acc_ref[...] + jnp.dot(s5_ref[...], wout_ref[0:ssm_w, :], preferred_element_type=F32)
    o_ref[...] = out.reshape(nb, tc, d)


def _mix0_call(h, g, win, bblk, ar, ai, cblk, dsk, wglu, bglu, ws, bs, gv, wout, *, nb=8, tc=128):
    b, s, d = h.shape
    ssm_w = wglu.shape[0]
    gm_w = gv.shape[1]
    nslab = bblk.shape[1] // LANES
    pitch = tc + SUBLANES
    rows = nb * tc
    kern = functools.partial(_mix0_kernel, nb=nb, tc=tc, pitch=pitch, d=d, ssm_w=ssm_w, gm_w=gm_w)
    return pl.pallas_call(
        kern,
        grid=(b // nb, s // tc),
        in_specs=[
            pl.BlockSpec((nb, tc, d), lambda bi, ti: (bi, ti, 0)),
            _const_spec((1, d)),
            _const_spec(win.shape),
            _const_spec(bblk.shape),
            _const_spec(ar.shape),
            _const_spec(ai.shape),
            _const_spec(cblk.shape),
            _const_spec((1, ssm_w)),
            _const_spec(wglu.shape),
            _const_spec((1, ssm_w)),
            _const_spec(ws.shape),
            _const_spec(bs.shape),
            _const_spec((1, gm_w)),
            _const_spec(wout.shape),
        ],
        out_specs=pl.BlockSpec((nb, tc, d), lambda bi, ti: (bi, ti, 0)),
        out_shape=jax.ShapeDtypeStruct((b, s, d), F32),
        scratch_shapes=[
            pltpu.VMEM((rows, d), BF16),
            pltpu.VMEM((ssm_w // LANES, nb * pitch, LANES), F32),
            pltpu.VMEM((rows, ssm_w), F32),
            pltpu.VMEM((nslab, rows, LANES), F32),
            pltpu.VMEM((nslab, nb, LANES), F32),
            pltpu.VMEM((ssm_w // LANES, rows, LANES), F32),
            pltpu.VMEM((rows, gm_w), BF16),
            pltpu.VMEM((rows, gm_w), BF16),
            pltpu.VMEM((rows, ssm_w), BF16),
            pltpu.VMEM((rows, d), F32),
        ],
        compiler_params=pltpu.CompilerParams(
            dimension_semantics=("arbitrary", "arbitrary"),
            vmem_limit_bytes=VMEM_LIMIT),
        name="s5_gmlp_mixer",
    )(h, g, win, bblk, ar, ai, cblk, dsk, wglu, bglu, ws, bs, gv, wout)


def _s5_operands(lam_re, lam_im, log_dt, b_re, b_im, c_re, c_im):
    g, p = lam_re.shape
    hdim = b_re.shape[-1]
    n = g * p
    npair = n // LANES
    ab_re, ab_im, bb_re, bb_im = _s5_prep(lam_re, lam_im, log_dt, b_re, b_im)
    eye = jnp.eye(g, dtype=F32)

    def b_dense(bb):
        bb = bb.reshape(g, p, hdim)
        return jnp.einsum('gph,gk->ghkp', bb, eye).reshape(g * hdim, npair, LANES)

    def c_dense(cm):
        return jnp.einsum('ghp,gk->gpkh', cm, eye).reshape(npair, LANES, g * hdim)

    bblk = jnp.stack([b_dense(bb_re), b_dense(bb_im)], axis=2).reshape(g * hdim, 2 * n)
    cblk = jnp.stack([c_dense(c_re), -c_dense(c_im)], axis=1).reshape(2 * n, g * hdim)
    ar = ab_re.reshape(npair, 1, LANES)
    ai = ab_im.reshape(npair, 1, LANES)
    return bblk.astype(BF16), ar, ai, cblk.astype(BF16)


def kernel(x, mix_norm_g, ffn_norm_g, final_norm_g, ev_w_in, ev_w_out, s5_lam_re, s5_lam_im,
           s5_log_dt, s5_b_re, s5_b_im, s5_c_re, s5_c_im, s5_d, s5_w_glu, s5_b_glu, gm_w_s,
           gm_b_s, gm_v_g, od_w_in, od_conv_w, od_conv_b, od_w_out, ffn_w_up, ffn_conv_w,
           ffn_conv_b, ffn_w_down):
    depth = mix_norm_g.shape[0]
    row = lambda a: a.reshape(1, -1).astype(F32)

    def col_tiles(w, parts, tf):
        k, n = w.shape[0], w.shape[1] // parts
        return w.reshape(k, parts, n // tf, tf).transpose(2, 0, 1, 3).reshape(n // tf, k, parts * tf)

    def conv_tiles(cw, cb, parts, tf):
        t = col_tiles(jnp.concatenate([cw, cb[None, :]], axis=0).astype(F32), parts, tf)
        return jnp.broadcast_to(t[:, :, None, :], t.shape[:2] + (SUBLANES, t.shape[2]))

    ffn_up, ffn_down = ffn_w_up.astype(BF16), ffn_w_down.astype(BF16)
    h = x
    for layer in range(depth):
        if layer % 2 == 0:
            e = layer // 2
            bblk, ar, ai, cblk = _s5_operands(s5_lam_re[e], s5_lam_im[e], s5_log_dt[e],
                                              s5_b_re[e], s5_b_im[e], s5_c_re[e], s5_c_im[e])
            h = _mix0_call(h, row(mix_norm_g[layer]), ev_w_in[e].astype(BF16), bblk, ar, ai, cblk,
                           row(s5_d[e]), s5_w_glu[e].astype(BF16), row(s5_b_glu[e]),
                           gm_w_s[e], gm_b_s[e][:, :, None], row(gm_v_g[e]),
                           ev_w_out[e].astype(BF16))
        else:
            o = layer // 2
            h = _shortconv_call(h, row(mix_norm_g[layer]),
                                od_w_in[o].astype(BF16),
                                conv_tiles(od_conv_w[o], od_conv_b[o], 1, MIX_TILE),
                                od_w_out[o].astype(BF16))
        gf = row(final_norm_g) if layer == depth - 1 else None
        h = _ffn_call(h, row(ffn_norm_g[layer]), ffn_up, conv_tiles(ffn_conv_w[layer], ffn_conv_b[layer], 2, FFN_TILE),
                      ffn_down, layer, gf)
    return h
```
